```python
import jax, jax.numpy as jnp
from jax import lax
import numpy as np

D_MODEL = 4096
BATCH = 2
SEQ = 8192
DEPTH = 2

CONV_CH = D_MODEL // 4
CONV_WIDTH = 31
SCONV_CH = D_MODEL // 4
SCONV_WIDTH = 3
ATTN_CH = D_MODEL // 4
HEAD_DIM = 64
N_Q_HEADS = ATTN_CH // HEAD_DIM
N_KV_HEADS = N_Q_HEADS // 8
GQA_GROUP = N_Q_HEADS // N_KV_HEADS
WINDOW = 128
BLOCK = 128
POOL_CH = D_MODEL // 4
POOL_WINDOWS = (2, 4, 8, 16)
POOL_GROUP = POOL_CH // len(POOL_WINDOWS)
D_FF = 4 * D_MODEL
PLE_DIM = 256
EPS = 1e-6

A_COLS = 2 * CONV_CH
B_COLS = 3 * SCONV_CH
Q_COLS = N_Q_HEADS * HEAD_DIM
KV_COLS = N_KV_HEADS * HEAD_DIM
D_COLS = POOL_CH
SPLIT_POINTS = (A_COLS, A_COLS + B_COLS, A_COLS + B_COLS + Q_COLS,
                A_COLS + B_COLS + Q_COLS + KV_COLS,
                A_COLS + B_COLS + Q_COLS + 2 * KV_COLS)
IN_COLS = A_COLS + B_COLS + Q_COLS + 2 * KV_COLS + D_COLS

kernel_name = "hybrid_parallel_conformer_shortconv_swa_pool"


def rmsnorm(x, g):
    xf = x.astype(jnp.float32)
    y = xf * lax.rsqrt(jnp.mean(xf * xf, axis=-1, keepdims=True) + EPS)
    return (y * g.astype(jnp.float32)).astype(x.dtype)


def rms_scale(x):
    xf = x.astype(jnp.float32)
    return (xf * lax.rsqrt(jnp.mean(xf * xf, axis=-1, keepdims=True) + EPS)).astype(x.dtype)


def layernorm(x, g, b):
    xf = x.astype(jnp.float32)
    mu = jnp.mean(xf, axis=-1, keepdims=True)
    var = jnp.mean(jnp.square(xf - mu), axis=-1, keepdims=True)
    y = (xf - mu) * lax.rsqrt(var + EPS)
    return (y * g.astype(jnp.float32) + b.astype(jnp.float32)).astype(x.dtype)


def depthwise_causal_conv(u, w):
    k = w.shape[0]
    return lax.conv_general_dilated(
        u, w[:, None, :].astype(u.dtype), window_strides=(1,), padding=[(k - 1, 0)],
        dimension_numbers=("NWC", "WIO", "NWC"), feature_group_count=u.shape[-1])


def conformer_conv(za, conv_w, conv_b, ln_g, ln_b):
    val, gate = jnp.split(za, 2, axis=-1)
    u = val * jax.nn.sigmoid(gate)
    u = depthwise_causal_conv(u, conv_w) + conv_b
    u = layernorm(u, ln_g, ln_b)
    return jax.nn.silu(u)


def short_gated_conv(zb, sconv_w):
    bg, cg, xin = jnp.split(zb, 3, axis=-1)
    return bg * depthwise_causal_conv(cg * xin, sconv_w)


def alibi_slopes():
    return jnp.exp2(-8.0 * jnp.arange(1, N_Q_HEADS + 1, dtype=jnp.float32) / N_Q_HEADS)


def sliding_window_attention(q, k, v, sinks):
    b, s = q.shape[:2]
    nb = s // BLOCK
    qb = q.reshape(b, nb, BLOCK, N_KV_HEADS, GQA_GROUP, HEAD_DIM)
    kb = k.reshape(b, nb, BLOCK, N_KV_HEADS, HEAD_DIM)
    vb = v.reshape(b, nb, BLOCK, N_KV_HEADS, HEAD_DIM)
    pad = ((0, 0), (1, 0), (0, 0), (0, 0), (0, 0))
    kk = jnp.concatenate([jnp.pad(kb, pad)[:, :-1], kb], axis=2)
    vv = jnp.concatenate([jnp.pad(vb, pad)[:, :-1], vb], axis=2)
    scores = jnp.einsum("bnqhgd,bnkhd->bnhgqk", qb, kk).astype(jnp.float32)
    scores = scores * (1.0 / np.sqrt(HEAD_DIM)).astype(np.float32)
    qi = jnp.arange(BLOCK)[:, None]
    kj = jnp.arange(2 * BLOCK)[None, :]
    dist = qi + BLOCK - kj
    in_window = (dist >= 0) & (dist < WINDOW)
    before_start = (jnp.arange(nb)[:, None, None] == 0) & (kj < BLOCK)[None]
    valid = in_window[None] & ~before_start
    alibi = -alibi_slopes()[:, None, None] * dist.astype(jnp.float32)[None]
    scores = scores + alibi.reshape(N_KV_HEADS, GQA_GROUP, BLOCK, 2 * BLOCK)
    scores = jnp.where(valid[None, :, None, None], scores, jnp.float32(-1e30))
    sink = jnp.broadcast_to(
        sinks.astype(jnp.float32).reshape(1, 1, N_KV_HEADS, GQA_GROUP, 1, 1),
        scores.shape[:-1] + (1,))
    probs = jax.nn.softmax(jnp.concatenate([scores, sink], axis=-1), axis=-1)[..., :-1]
    out = jnp.einsum("bnhgqk,bnkhd->bnqhgd", probs.astype(vv.dtype), vv)
    return out.reshape(b, s, N_Q_HEADS * HEAD_DIM)


def multiscale_pool(u, pool_w, pool_scale):
    b, s = u.shape[:2]
    uf = u.astype(jnp.float32).reshape(b, s, len(POOL_WINDOWS), POOL_GROUP)
    cs = jnp.pad(jnp.cumsum(uf, axis=1), ((0, 0), (1, 0), (0, 0), (0, 0)))
    t = jnp.arange(s)
    groups = []
    for gi, w in enumerate(POOL_WINDOWS):
        seg = cs[:, :, gi]
        start = jnp.maximum(t + 1 - w, 0)
        count = jnp.minimum(t + 1, w).astype(jnp.float32)[None, :, None]
        groups.append((seg[:, 1:] - seg[:, start]) / count - uf[:, :, gi])
    z = jnp.stack(groups, axis=2).astype(u.dtype)
    z = jnp.einsum("bsgc,gcd->bsgd", z, pool_w).reshape(b, s, POOL_CH)
    return z * pool_scale


def setup_inputs(seed: int = 0) -> dict:
    key = jax.random.key(seed)
    ks = jax.random.split(key, 24)
    f32 = jnp.float32

    def nrm(k, shape, scale):
        return jax.random.normal(k, shape, f32) * scale

    def gain(k, shape):
        return 1.0 + 0.05 * jax.random.normal(k, shape, f32)

    return {
        "x": nrm(ks[0], (BATCH, SEQ, D_MODEL), 1.0),
        "p": nrm(ks[1], (DEPTH, BATCH, SEQ, PLE_DIM), 1.0),
        "norm_mix": gain(ks[2], (DEPTH, D_MODEL)),
        "w_in": nrm(ks[3], (DEPTH, D_MODEL, IN_COLS), D_MODEL ** -0.5),
        "conv_w": nrm(ks[4], (DEPTH, CONV_WIDTH, CONV_CH), CONV_WIDTH ** -0.5),
        "conv_b": nrm(ks[5], (DEPTH, CONV_CH), 0.02),
        "conv_ln_g": gain(ks[6], (DEPTH, CONV_CH)),
        "conv_ln_b": nrm(ks[7], (DEPTH, CONV_CH), 0.02),
        "sconv_w": nrm(ks[8], (DEPTH, SCONV_WIDTH, SCONV_CH), SCONV_WIDTH ** -0.5),
        "attn_sinks": nrm(ks[9], (DEPTH, N_Q_HEADS), 0.5),
        "pool_w": nrm(ks[10], (DEPTH, len(POOL_WINDOWS), POOL_GROUP, POOL_GROUP), POOL_GROUP ** -0.5),
        "pool_scale": gain(ks[11], (DEPTH, POOL_CH)),
        "mix_gain": gain(ks[12], (DEPTH, D_MODEL)),
        "w_out": nrm(ks[13], (DEPTH, D_MODEL, D_MODEL), D_MODEL ** -0.5),
        "norm_mlp": gain(ks[14], (DEPTH, D_MODEL)),
        "w_up": nrm(ks[15], (DEPTH, D_MODEL, D_FF), D_MODEL ** -0.5),
        "w_down": nrm(ks[16], (DEPTH, D_FF, D_MODEL), D_FF ** -0.5),
        "w_ple_gate": nrm(ks[17], (DEPTH, D_MODEL, D_MODEL), D_MODEL ** -0.5),
        "w_ple_proj": nrm(ks[18], (DEPTH, PLE_DIM, D_MODEL), PLE_DIM ** -0.5),
        "norm_final": gain(ks[19], (D_MODEL,)),
    }


def reference(x, p, norm_mix, w_in, conv_w, conv_b, conv_ln_g, conv_ln_b, sconv_w,
              attn_sinks, pool_w, pool_scale, mix_gain, w_out, norm_mlp, w_up, w_down,
              w_ple_gate, w_ple_proj, norm_final):
    h = x
    for i in range(DEPTH):
        hn = rmsnorm(h, norm_mix[i])
        proj = jnp.einsum("bsd,dc->bsc", hn, w_in[i])
        za, zb, zq, zk, zv, zd = jnp.split(proj, SPLIT_POINTS, axis=-1)
        ya = conformer_conv(za, conv_w[i], conv_b[i], conv_ln_g[i], conv_ln_b[i])
        yb = short_gated_conv(zb, sconv_w[i])
        yc = sliding_window_attention(zq, zk, zv, attn_sinks[i])
        yd = multiscale_pool(zd, pool_w[i], pool_scale[i])
        y = jnp.concatenate([rms_scale(ya), rms_scale(yb), rms_scale(yc), rms_scale(yd)],
                            axis=-1) * mix_gain[i]
        h = h + jnp.einsum("bsc,cd->bsd", y, w_out[i])
        hn = rmsnorm(h, norm_mlp[i])
        up = jnp.square(jax.nn.relu(jnp.einsum("bsd,df->bsf", hn, w_up[i])))
        h = h + jnp.einsum("bsf,fd->bsd", up, w_down[i])
        gate = jax.nn.sigmoid(jnp.einsum("bsd,de->bse", h, w_ple_gate[i]))
        h = h + gate * jnp.einsum("bsk,kd->bsd", p[i], w_ple_proj[i])
    return rmsnorm(h, norm_final)
```

```python
import functools

import numpy as np
import jax
import jax.numpy as jnp
from jax import lax
from jax.experimental import pallas as pl
from jax.experimental.pallas import tpu as pltpu

F32 = jnp.float32
BF16 = jnp.bfloat16
EPS = 1e-6

LANES = 128
BF16_ROWS = 16
VMEM_LIMIT_BYTES = 56 * 1024 * 1024

CONV_WIDTH = 31
SCONV_WIDTH = 3
HEAD_DIM = 64
N_Q_HEADS = 16
WINDOW = 128
POOL_WINDOWS = (2, 4, 8, 16)
HALO = 32
NEG = -1e30


def _params(sem):
    return pltpu.CompilerParams(dimension_semantics=sem, vmem_limit_bytes=VMEM_LIMIT_BYTES)


def _rmsnorm_rows(src_ref, g_ref, dst_ref, n_rows, chunk, copy_ref=None):
    def body(r, carry):
        rows = pl.ds(pl.multiple_of(r * chunk, chunk), chunk)
        x = src_ref[rows, :].astype(F32)
        ms = jnp.mean(x * x, axis=-1, keepdims=True)
        dst_ref[rows, :] = (x * lax.rsqrt(ms + EPS) * g_ref[...]).astype(dst_ref.dtype)
        if copy_ref is not None:
            copy_ref[rows, :] = x
        return carry

    lax.fori_loop(0, n_rows // chunk, body, 0)


def _inproj_kernel(h_ref, g_ref, w_ref, o_ref, hn_ref, *, tm):
    @pl.when(pl.program_id(1) == 0)
    def _():
        _rmsnorm_rows(h_ref, g_ref, hn_ref, tm, BF16_ROWS)

    o_ref[...] = jnp.dot(hn_ref[...], w_ref[...], preferred_element_type=F32).astype(o_ref.dtype)


def _inproj(h, g, w, *, tm, tn):
    t, d = h.shape
    n = w.shape[1]
    return pl.pallas_call(
        functools.partial(_inproj_kernel, tm=tm),
        grid=(t // tm, n // tn),
        in_specs=[
            pl.BlockSpec((tm, d), lambda i, j: (i, 0)),
            pl.BlockSpec((1, d), lambda i, j: (0, 0)),
            pl.BlockSpec((d, tn), lambda i, j: (0, j)),
        ],
        out_specs=pl.BlockSpec((tm, tn), lambda i, j: (i, j)),
        out_shape=jax.ShapeDtypeStruct((t, n), BF16),
        scratch_shapes=[pltpu.VMEM((tm, d), BF16)],
        compiler_params=_params(("parallel", "arbitrary")),
        name="inproj",
    )(h, g, w)


def _mixer_kernel(cur_ref, prev_ref, conv_w_ref, conv_b_ref, ln_g_ref, ln_b_ref, sconv_w_ref,
                  sinks_ref, bias_ref, pool_w_ref, pool_scale_ref, gain_ref, y_ref,
                  ext_ref, tmp_ref, kd_ref, vd_ref, *, ts, ch):
    i = pl.program_id(1)
    not_first = jnp.where(i > 0, 1.0, 0.0).astype(F32)
    n_lane_chunks = ch // LANES
    col_a, col_b, col_q = 0, 2 * ch, 5 * ch
    col_k = 6 * ch
    col_v = col_k + LANES
    col_d = col_v + LANES
    hrows = slice(ts - HALO, ts)

    def cur(c0, n=LANES):
        return cur_ref[0, :, c0:c0 + n]

    def prev_halo(c0, n=LANES):
        return prev_ref[0, hrows, c0:c0 + n].astype(F32)

    def finish(x_rows, col0, rows):
        ms = jnp.mean(x_rows * x_rows, axis=-1, keepdims=True)
        g = gain_ref[:, col0:col0 + ch]
        y_ref[0, rows, col0:col0 + ch] = (x_rows * lax.rsqrt(ms + EPS) * g).astype(y_ref.dtype)

    sub = 32
    row_chunks = [slice(r, r + sub) for r in range(0, ts, sub)]

    for c in range(n_lane_chunks):
        l0 = c * LANES
        val = cur(col_a + l0).astype(F32)
        gate = cur(col_a + ch + l0).astype(F32)
        ext_ref[HALO:HALO + ts, l0:l0 + LANES] = val * jax.nn.sigmoid(gate)
        pval = prev_halo(col_a + l0)
        pgate = prev_halo(col_a + ch + l0)
        ext_ref[0:HALO, l0:l0 + LANES] = pval * jax.nn.sigmoid(pgate) * not_first
    for c in range(n_lane_chunks):
        l0 = c * LANES
        acc = jnp.zeros((ts, LANES), F32)
        for k in range(CONV_WIDTH):
            off = HALO - (CONV_WIDTH - 1) + k
            acc = acc + ext_ref[off:off + ts, l0:l0 + LANES] * conv_w_ref[k:k + 1, l0:l0 + LANES]
        tmp_ref[:, l0:l0 + LANES] = acc + conv_b_ref[:, l0:l0 + LANES]
    for rows in row_chunks:
        x = tmp_ref[rows, :]
        mu = jnp.mean(x, axis=-1, keepdims=True)
        xc = x - mu
        var = jnp.mean(xc * xc, axis=-1, keepdims=True)
        ln = xc * lax.rsqrt(var + EPS) * ln_g_ref[...] + ln_b_ref[...]
        finish(ln * jax.nn.sigmoid(ln), 0, rows)

    for c in range(n_lane_chunks):
        l0 = c * LANES
        cg = cur(col_b + ch + l0).astype(F32)
        xin = cur(col_b + 2 * ch + l0).astype(F32)
        ext_ref[HALO:HALO + ts, l0:l0 + LANES] = cg * xin
        ext_ref[0:HALO, l0:l0 + LANES] = (prev_halo(col_b + ch + l0) * prev_halo(col_b + 2 * ch + l0)
                                          * not_first)
    for c in range(n_lane_chunks):
        l0 = c * LANES
        acc = jnp.zeros((ts, LANES), F32)
        for k in range(SCONV_WIDTH):
            off = HALO - (SCONV_WIDTH - 1) + k
            acc = acc + ext_ref[off:off + ts, l0:l0 + LANES] * sconv_w_ref[k:k + 1, l0:l0 + LANES]
        tmp_ref[:, l0:l0 + LANES] = cur(col_b + l0).astype(F32) * acc
    for rows in row_chunks:
        finish(tmp_ref[rows, :], ch, rows)

    lane = lax.broadcasted_iota(jnp.int32, (2 * ts, LANES), 1)
    lo = lane < HEAD_DIM

    def block_diag(col0, dst_ref):
        both = jnp.concatenate([prev_ref[0, :, col0:col0 + LANES], cur(col0)], axis=0).astype(F32)
        swapped = pltpu.roll(both, HEAD_DIM, 1)
        zero = jnp.zeros_like(both)
        dst_ref[0, 0:2 * ts, :] = jnp.where(lo, both, zero).astype(BF16)
        dst_ref[0, 2 * ts:4 * ts, :] = jnp.where(lo, zero, swapped).astype(BF16)
        dst_ref[1, 0:2 * ts, :] = jnp.where(lo, swapped, zero).astype(BF16)
        dst_ref[1, 2 * ts:4 * ts, :] = jnp.where(lo, zero, both).astype(BF16)

    block_diag(col_k, kd_ref)
    block_diag(col_v, vd_ref)
    kcol = lax.broadcasted_iota(jnp.int32, (ts, 4 * ts), 1)
    before_start = (kcol & (2 * ts - 1)) < jnp.where(i == 0, ts, 0)
    lo_q = lax.broadcasted_iota(jnp.int32, (ts, LANES), 1) < HEAD_DIM
    heads_per_kv = N_Q_HEADS // 2
    for pair in range(N_Q_HEADS // 2):
        kvh = (2 * pair) // heads_per_kv
        q2 = cur(col_q + pair * LANES)
        s = lax.dot_general(q2, kd_ref[kvh], (((1,), (1,)), ((), ())), preferred_element_type=F32)
        s = s * (1.0 / np.sqrt(HEAD_DIM)).astype(np.float32) + bias_ref[pair]
        s = jnp.where(before_start, NEG, s)
        probs, inv = [], []
        for hh in range(2):
            sh = s[:, hh * 2 * ts:(hh + 1) * 2 * ts]
            sink = sinks_ref[2 * pair + hh]
            m = jnp.maximum(jnp.max(sh, axis=-1, keepdims=True), sink)
            e = jnp.exp(sh - m)
            den = jnp.sum(e, axis=-1, keepdims=True) + jnp.exp(sink - m)
            probs.append(e.astype(BF16))
            inv.append(1.0 / den)
        o = jnp.dot(jnp.concatenate(probs, axis=1), vd_ref[kvh], preferred_element_type=F32)
        tmp_ref[:, pair * LANES:(pair + 1) * LANES] = o * jnp.where(lo_q, inv[0], inv[1])
    for rows in row_chunks:
        finish(tmp_ref[rows, :], 2 * ch, rows)

    grp = ch // len(POOL_WINDOWS)
    t_glob = i * ts + lax.broadcasted_iota(jnp.int32, (ts, 1), 0)
    for gi, w in enumerate(POOL_WINDOWS):
        l0 = gi * grp
        d_cur = cur(col_d + l0, grp).astype(F32)
        ext_ref[HALO:HALO + ts, l0:l0 + grp] = d_cur
        ext_ref[0:HALO, l0:l0 + grp] = prev_halo(col_d + l0, grp) * not_first
        shift = 1
        while shift < w:
            hi = ext_ref[8:HALO + ts, l0:l0 + grp]
            sh_ = ext_ref[8 - shift:HALO + ts - shift, l0:l0 + grp]
            ext_ref[8:HALO + ts, l0:l0 + grp] = hi + sh_
            shift *= 2
        count = jnp.minimum(t_glob + 1, w).astype(F32)
        z = ext_ref[HALO:HALO + ts, l0:l0 + grp] / count - d_cur
        zz = jnp.dot(z.astype(BF16), pool_w_ref[gi], preferred_element_type=F32)
        tmp_ref[:, l0:l0 + grp] = zz * pool_scale_ref[:, l0:l0 + grp]
    for rows in row_chunks:
        finish(tmp_ref[rows, :], 3 * ch, rows)


def _attn_bias_pairs(ts):
    qi = np.arange(ts)[:, None]
    kj = np.arange(2 * ts)[None, :]
    dist = qi + ts - kj
    valid = (dist >= 0) & (dist < WINDOW)
    slopes = np.exp2(-8.0 * np.arange(1, N_Q_HEADS + 1, dtype=np.float32) / N_Q_HEADS).astype(np.float32)
    bias = np.where(valid[None], -slopes[:, None, None] * dist.astype(np.float32)[None], np.float32(NEG))
    bias = bias.astype(np.float32).reshape(N_Q_HEADS // 2, 2, ts, 2 * ts)
    return np.concatenate([bias[:, 0], bias[:, 1]], axis=-1)


def _mixers(proj, conv_w, conv_b, ln_g, ln_b, sconv_w, sinks, pool_w, pool_scale, gain, *, d):
    b, s, n = proj.shape
    ts = WINDOW
    ch = d // 4
    bias = jnp.asarray(_attn_bias_pairs(ts))
    full = lambda shape: pl.BlockSpec(shape, lambda bi, i: (0,) * len(shape))
    return pl.pallas_call(
        functools.partial(_mixer_kernel, ts=ts, ch=ch),
        grid=(b, s // ts),
        in_specs=[
            pl.BlockSpec((1, ts, n), lambda bi, i: (bi, i, 0)),
            pl.BlockSpec((1, ts, n), lambda bi, i: (bi, jnp.maximum(i - 1, 0), 0)),
            full((CONV_WIDTH, ch)), full((1, ch)), full((1, ch)), full((1, ch)),
            full((SCONV_WIDTH, ch)),
            pl.BlockSpec(memory_space=pltpu.SMEM),
            full(bias.shape),
            full(pool_w.shape), full((1, ch)), full((1, d)),
        ],
        out_specs=pl.BlockSpec((1, ts, d), lambda bi, i: (bi, i, 0)),
        out_shape=jax.ShapeDtypeStruct((b, s, d), BF16),
        scratch_shapes=[pltpu.VMEM((HALO + ts, ch), F32), pltpu.VMEM((ts, ch), F32),
                        pltpu.VMEM((2, 4 * ts, LANES), BF16), pltpu.VMEM((2, 4 * ts, LANES), BF16)],
        compiler_params=_params(("parallel", "arbitrary")),
        name="mixers",
    )(proj, proj, conv_w, conv_b, ln_g, ln_b, sconv_w, sinks, bias, pool_w, pool_scale, gain)


def _outproj_kernel(y_ref, w_ref, h_ref, o_ref):
    o_ref[...] = h_ref[...] + jnp.dot(y_ref[...], w_ref[...], preferred_element_type=F32)


def _outproj(y, w, h, *, tm, tn):
    t, d = h.shape
    k = y.shape[1]
    return pl.pallas_call(
        _outproj_kernel,
        grid=(t // tm, d // tn),
        in_specs=[
            pl.BlockSpec((tm, k), lambda i, j: (i, 0)),
            pl.BlockSpec((k, tn), lambda i, j: (0, j)),
            pl.BlockSpec((tm, tn), lambda i, j: (i, j)),
        ],
        out_specs=pl.BlockSpec((tm, tn), lambda i, j: (i, j)),
        out_shape=jax.ShapeDtypeStruct((t, d), F32),
        compiler_params=_params(("parallel", "arbitrary")),
        name="outproj",
    )(y, w, h)


def _mlp_kernel(h_ref, g_ref, wu_ref, wd_ref, o_ref, hn_ref, *, tm):
    @pl.when(pl.program_id(1) == 0)
    def _():
        _rmsnorm_rows(h_ref, g_ref, hn_ref, tm, BF16_ROWS, copy_ref=o_ref)

    up = jnp.dot(hn_ref[...], wu_ref[...], preferred_element_type=F32)
    up = jnp.square(jnp.maximum(up, 0.0)).astype(BF16)
    o_ref[...] += jnp.dot(up, wd_ref[...], preferred_element_type=F32)


def _mlp(h, g, wu, wd, *, tm, tf):
    t, d = h.shape
    f = wu.shape[1]
    return pl.pallas_call(
        functools.partial(_mlp_kernel, tm=tm),
        grid=(t // tm, f // tf),
        in_specs=[
            pl.BlockSpec((tm, d), lambda i, j: (i, 0)),
            pl.BlockSpec((1, d), lambda i, j: (0, 0)),
            pl.BlockSpec((d, tf), lambda i, j: (0, j)),
            pl.BlockSpec((tf, d), lambda i, j: (j, 0)),
        ],
        out_specs=pl.BlockSpec((tm, d), lambda i, j: (i, 0)),
        out_shape=jax.ShapeDtypeStruct((t, d), F32),
        scratch_shapes=[pltpu.VMEM((tm, d), BF16)],
        compiler_params=_params(("parallel", "arbitrary")),
        name="mlp",
    )(h, g, wu, wd)


def _ple_kernel(h_ref, hcol_ref, p_ref, wg_ref, wp_ref, o_ref, hb_ref, *, tm):
    @pl.when(pl.program_id(1) == 0)
    def _():
        def body(r, carry):
            rows = pl.ds(pl.multiple_of(r * BF16_ROWS, BF16_ROWS), BF16_ROWS)
            hb_ref[rows, :] = h_ref[rows, :].astype(BF16)
            return carry

        lax.fori_loop(0, tm // BF16_ROWS, body, 0)

    gate = jax.nn.sigmoid(jnp.dot(hb_ref[...], wg_ref[...], preferred_element_type=F32))
    emb = jnp.dot(p_ref[...].astype(BF16), wp_ref[...], preferred_element_type=F32)
    o_ref[...] = hcol_ref[...] + gate * emb


def _ple(h, p, wg, wp, *, tm, tn):
    t, d = h.shape
    kp = p.shape[1]
    return pl.pallas_call(
        functools.partial(_ple_kernel, tm=tm),
        grid=(t // tm, d // tn),
        in_specs=[
            pl.BlockSpec((tm, d), lambda i, j: (i, 0)),
            pl.BlockSpec((tm, tn), lambda i, j: (i, j)),
            pl.BlockSpec((tm, kp), lambda i, j: (i, 0)),
            pl.BlockSpec((d, tn), lambda i, j: (0, j)),
            pl.BlockSpec((kp, tn), lambda i, j: (0, j)),
        ],
        out_specs=pl.BlockSpec((tm, tn), lambda i, j: (i, j)),
        out_shape=jax.ShapeDtypeStruct((t, d), F32),
        scratch_shapes=[pltpu.VMEM((tm, d), BF16)],
        compiler_params=_params(("parallel", "arbitrary")),
        name="ple",
    )(h, h, p, wg, wp)


def _final_norm_kernel(h_ref, g_ref, o_ref, *, tm):
    _rmsnorm_rows(h_ref, g_ref, o_ref, tm, 8)


def _final_norm(h, g, *, tm):
    t, d = h.shape
    return pl.pallas_call(
        functools.partial(_final_norm_kernel, tm=tm),
        grid=(t // tm,),
        in_specs=[pl.BlockSpec((tm, d), lambda i: (i, 0)), pl.BlockSpec((1, d), lambda i: (0, 0))],
        out_specs=pl.BlockSpec((tm, d), lambda i: (i, 0)),
        out_shape=jax.ShapeDtypeStruct((t, d), F32),
        compiler_params=_params(("parallel",)),
        name="final_norm",
    )(h, g)


def kernel(x, p, norm_mix, w_in, conv_w, conv_b, conv_ln_g, conv_ln_b, sconv_w, attn_sinks, pool_w,
           pool_scale, mix_gain, w_out, norm_mlp, w_up, w_down, w_ple_gate, w_ple_proj, norm_final):
    b, s, d = x.shape
    depth = w_in.shape[0]
    t = b * s
    in_cols = w_in.shape[2]
    tn_in = 768
    in_cols_padded = -(-in_cols // tn_in) * tn_in
    row = lambda v: v.reshape(1, -1).astype(F32)

    tm_small, tm_big = min(512, t), min(1024, t)
    h = x.reshape(t, d)
    for i in range(depth):
        w_in_b = jnp.pad(w_in[i].astype(BF16), ((0, 0), (0, in_cols_padded - in_cols)))
        proj = _inproj(h, row(norm_mix[i]), w_in_b, tm=tm_small, tn=tn_in)
        y = _mixers(proj.reshape(b, s, in_cols_padded), conv_w[i], row(conv_b[i]), row(conv_ln_g[i]),
                    row(conv_ln_b[i]), sconv_w[i], attn_sinks[i], pool_w[i].astype(BF16),
                    row(pool_scale[i]), row(mix_gain[i]), d=d)
        h = _outproj(y.reshape(t, d), w_out[i].astype(BF16), h, tm=tm_big, tn=512)
        h = _mlp(h, row(norm_mlp[i]), w_up[i].astype(BF16), w_down[i].astype(BF16), tm=tm_small, tf=256)
        h = _ple(h, p[i].reshape(t, -1), w_ple_gate[i].astype(BF16), w_ple_proj[i].astype(BF16),
                 tm=tm_small, tn=512)
    out = _final_norm(h, row(norm_final), tm=256)
    return out.reshape(b, s, d)
```

```python
import functools

import numpy as np
import jax
import jax.numpy as jnp
from jax import lax
from jax.experimental import pallas as pl
from jax.experimental.pallas import tpu as pltpu

F32 = jnp.float32
BF16 = jnp.bfloat16
EPS = 1e-6

LANES = 128
BF16_ROWS = 16
SUBLANES = 8
VMEM_LIMIT_BYTES = 56 * 1024 * 1024
CAST_BLOCK_BYTES = 4 * 1024 * 1024

CONV_WIDTH = 31
SCONV_WIDTH = 3
HEAD_DIM = 64
N_Q_HEADS = 16
WINDOW = 128
POOL_WINDOWS = (2, 4, 8, 16)
HALO = 32
NEG = -1e30


def _params(sem):
    return pltpu.CompilerParams(dimension_semantics=sem, vmem_limit_bytes=VMEM_LIMIT_BYTES)


def _rmsnorm_rows(src_ref, g_ref, dst_ref, n_rows, copy_ref=None):
    d = src_ref.shape[1]
    lane_chunks = [slice(c, c + LANES) for c in range(0, d, LANES)]

    def body(r, carry):
        rows = pl.ds(pl.multiple_of(r * BF16_ROWS, BF16_ROWS), BF16_ROWS)
        ss = [jnp.zeros((BF16_ROWS, LANES), F32) for _ in range(4)]
        for n, lanes in enumerate(lane_chunks):
            x = src_ref[rows, lanes]
            ss[n % 4] = ss[n % 4] + x * x
            if copy_ref is not None:
                copy_ref[rows, lanes] = x
        tot = jnp.sum((ss[0] + ss[1]) + (ss[2] + ss[3]), axis=-1, keepdims=True)
        scale = jnp.broadcast_to(lax.rsqrt(tot * (1.0 / d) + EPS), (BF16_ROWS, LANES))
        for lanes in lane_chunks:
            dst_ref[rows, lanes] = (src_ref[rows, lanes] * scale * g_ref[:, lanes]).astype(dst_ref.dtype)
        return carry

    lax.fori_loop(0, n_rows // BF16_ROWS, body, 0, unroll=2)


def _cast_kernel(w_ref, o_ref, *, n):
    o_ref[:, 0:n] = w_ref[...].astype(o_ref.dtype)
    if o_ref.shape[1] > n:
        o_ref[:, n:] = jnp.zeros((o_ref.shape[0], o_ref.shape[1] - n), o_ref.dtype)


def _cast_layer_bf16(w, layer, n_padded=None):
    _, k, n = w.shape
    n_padded = n if n_padded is None else n_padded
    rows_fit = CAST_BLOCK_BYTES // (4 * n)
    bk = min(k, max(BF16_ROWS, 1 << (rows_fit.bit_length() - 1)))
    return pl.pallas_call(
        functools.partial(_cast_kernel, n=n),
        grid=(k // bk,),
        in_specs=[pl.BlockSpec((None, bk, n), lambda r: (layer, r, 0))],
        out_specs=pl.BlockSpec((bk, n_padded), lambda r: (r, 0)),
        out_shape=jax.ShapeDtypeStruct((k, n_padded), BF16),
        compiler_params=_params(("parallel",)),
        name="cast_bf16",
    )(w)


def _inproj_kernel(h_ref, g_ref, w_ref, o_ref, hn_ref, *, tm):
    @pl.when(pl.program_id(1) == 0)
    def _():
        _rmsnorm_rows(h_ref, g_ref, hn_ref, tm)

    o_ref[...] = jnp.dot(hn_ref[...], w_ref[...], preferred_element_type=F32).astype(o_ref.dtype)


def _inproj(h, g, w, *, tm, tn):
    t, d = h.shape
    n = w.shape[1]
    return pl.pallas_call(
        functools.partial(_inproj_kernel, tm=tm),
        grid=(t // tm, n // tn),
        in_specs=[
            pl.BlockSpec((tm, d), lambda i, j: (i, 0)),
            pl.BlockSpec((BF16_ROWS, d), lambda i, j: (0, 0)),
            pl.BlockSpec((d, tn), lambda i, j: (0, j)),
        ],
        out_specs=pl.BlockSpec((tm, tn), lambda i, j: (i, j)),
        out_shape=jax.ShapeDtypeStruct((t, n), BF16),
        scratch_shapes=[pltpu.VMEM((tm, d), BF16)],
        compiler_params=_params(("parallel", "arbitrary")),
        name="inproj",
    )(h, g, w)


def _mixer_kernel(cur_ref, prev_ref, conv_w_ref, conv_b_ref, ln_g_ref, ln_b_ref, sconv_w_ref,
                  sinks_ref, bias_ref, pool_w_ref, pool_scale_ref, gain_ref, y_ref,
                  ext_ref, tmp_ref, kd_ref, vd_ref, *, ts, ch):
    i = pl.program_id(1)
    not_first = jnp.where(i > 0, 1.0, 0.0).astype(F32)
    n_lane_chunks = ch // LANES
    col_a, col_b, col_q = 0, 2 * ch, 5 * ch
    col_k = 6 * ch
    col_v = col_k + LANES
    col_d = col_v + LANES
    hrows = slice(ts - HALO, ts)

    def cur(c0, n=LANES):
        return cur_ref[0, :, c0:c0 + n]

    def prev_halo(c0, n=LANES):
        return prev_ref[0, hrows, c0:c0 + n].astype(F32)

    def finish(x_rows, col0, rows):
        ms = jnp.mean(x_rows * x_rows, axis=-1, keepdims=True)
        g = gain_ref[:, col0:col0 + ch]
        y_ref[0, rows, col0:col0 + ch] = (x_rows * lax.rsqrt(ms + EPS) * g).astype(y_ref.dtype)

    sub = 32
    row_chunks = [slice(r, r + sub) for r in range(0, ts, sub)]

    for c in range(n_lane_chunks):
        l0 = c * LANES
        val = cur(col_a + l0).astype(F32)
        gate = cur(col_a + ch + l0).astype(F32)
        ext_ref[HALO:HALO + ts, l0:l0 + LANES] = val * jax.nn.sigmoid(gate)
        pval = prev_halo(col_a + l0)
        pgate = prev_halo(col_a + ch + l0)
        ext_ref[0:HALO, l0:l0 + LANES] = pval * jax.nn.sigmoid(pgate) * not_first
    first = HALO - (CONV_WIDTH - 1)
    ext_ref[HALO + ts:HALO + ts + SUBLANES, :] = jnp.zeros((SUBLANES, ch), F32)
    for c in range(n_lane_chunks):
        l0 = c * LANES
        acc = conv_b_ref[:, l0:l0 + LANES]
        for b in range(SUBLANES):
            q = None
            for a in range(HALO // SUBLANES + 1):
                o = SUBLANES * a + b
                if first <= o <= HALO:
                    term = (ext_ref[SUBLANES * a:SUBLANES * a + ts + SUBLANES, l0:l0 + LANES]
                            * conv_w_ref[o - first:o - first + 1, l0:l0 + LANES])
                    q = term if q is None else q + term
            acc = acc + q[b:b + ts]
        tmp_ref[:, l0:l0 + LANES] = acc
    for rows in row_chunks:
        x = tmp_ref[rows, :]
        mu = jnp.mean(x, axis=-1, keepdims=True)
        xc = x - mu
        var = jnp.mean(xc * xc, axis=-1, keepdims=True)
        ln = xc * lax.rsqrt(var + EPS) * ln_g_ref[...] + ln_b_ref[...]
        finish(ln * jax.nn.sigmoid(ln), 0, rows)

    for c in range(n_lane_chunks):
        l0 = c * LANES
        cg = cur(col_b + ch + l0).astype(F32)
        xin = cur(col_b + 2 * ch + l0).astype(F32)
        ext_ref[HALO:HALO + ts, l0:l0 + LANES] = cg * xin
        ext_ref[0:HALO, l0:l0 + LANES] = (prev_halo(col_b + ch + l0) * prev_halo(col_b + 2 * ch + l0)
                                          * not_first)
    for c in range(n_lane_chunks):
        l0 = c * LANES
        acc = jnp.zeros((ts, LANES), F32)
        for k in range(SCONV_WIDTH):
            off = HALO - (SCONV_WIDTH - 1) + k
            acc = acc + ext_ref[off:off + ts, l0:l0 + LANES] * sconv_w_ref[k:k + 1, l0:l0 + LANES]
        tmp_ref[:, l0:l0 + LANES] = cur(col_b + l0).astype(F32) * acc
    for rows in row_chunks:
        finish(tmp_ref[rows, :], ch, rows)

    lane = lax.broadcasted_iota(jnp.int32, (2 * ts, LANES), 1)
    lo = lane < HEAD_DIM

    def block_diag(col0, dst_ref):
        both = jnp.concatenate([prev_ref[0, :, col0:col0 + LANES], cur(col0)], axis=0).astype(F32)
        swapped = pltpu.roll(both, HEAD_DIM, 1)
        zero = jnp.zeros_like(both)
        dst_ref[0, 0:2 * ts, :] = jnp.where(lo, both, zero).astype(BF16)
        dst_ref[0, 2 * ts:4 * ts, :] = jnp.where(lo, zero, swapped).astype(BF16)
        dst_ref[1, 0:2 * ts, :] = jnp.where(lo, swapped, zero).astype(BF16)
        dst_ref[1, 2 * ts:4 * ts, :] = jnp.where(lo, zero, both).astype(BF16)

    block_diag(col_k, kd_ref)
    block_diag(col_v, vd_ref)
    kcol = lax.broadcasted_iota(jnp.int32, (ts, 4 * ts), 1)
    before_start = (kcol & (2 * ts - 1)) < jnp.where(i == 0, ts, 0)
    lo_q = lax.broadcasted_iota(jnp.int32, (ts, LANES), 1) < HEAD_DIM
    heads_per_kv = N_Q_HEADS // 2
    for pair in range(N_Q_HEADS // 2):
        kvh = (2 * pair) // heads_per_kv
        q2 = cur(col_q + pair * LANES)
        s = lax.dot_general(q2, kd_ref[kvh], (((1,), (1,)), ((), ())), preferred_element_type=F32)
        s = s * (1.0 / np.sqrt(HEAD_DIM)).astype(np.float32) + bias_ref[pair]
        s = jnp.where(before_start, NEG, s)
        probs, inv = [], []
        for hh in range(2):
            sh = s[:, hh * 2 * ts:(hh + 1) * 2 * ts]
            sink = sinks_ref[2 * pair + hh]
            m = jnp.maximum(jnp.max(sh, axis=-1, keepdims=True), sink)
            e = jnp.exp(sh - m)
            den = jnp.sum(e, axis=-1, keepdims=True) + jnp.exp(sink - m)
            probs.append(e.astype(BF16))
            inv.append(1.0 / den)
        o = jnp.dot(jnp.concatenate(probs, axis=1), vd_ref[kvh], preferred_element_type=F32)
        tmp_ref[:, pair * LANES:(pair + 1) * LANES] = o * jnp.where(lo_q, inv[0], inv[1])
    for rows in row_chunks:
        finish(tmp_ref[rows, :], 2 * ch, rows)

    grp = ch // len(POOL_WINDOWS)
    t_glob = i * ts + lax.broadcasted_iota(jnp.int32, (ts, 1), 0)
    for gi, w in enumerate(POOL_WINDOWS):
        l0 = gi * grp
        d_cur = cur(col_d + l0, grp).astype(F32)
        ext_ref[HALO:HALO + ts, l0:l0 + grp] = d_cur
        ext_ref[0:HALO, l0:l0 + grp] = prev_halo(col_d + l0, grp) * not_first
        shift = 1
        while shift < w:
            hi = ext_ref[8:HALO + ts, l0:l0 + grp]
            sh_ = ext_ref[8 - shift:HALO + ts - shift, l0:l0 + grp]
            ext_ref[8:HALO + ts, l0:l0 + grp] = hi + sh_
            shift *= 2
        count = jnp.minimum(t_glob + 1, w).astype(F32)
        z = ext_ref[HALO:HALO + ts, l0:l0 + grp] / count - d_cur
        zz = jnp.dot(z.astype(BF16), pool_w_ref[gi], preferred_element_type=F32)
        tmp_ref[:, l0:l0 + grp] = zz * pool_scale_ref[:, l0:l0 + grp]
    for rows in row_chunks:
        finish(tmp_ref[rows, :], 3 * ch, rows)


def _attn_bias_pairs(ts):
    qi = np.arange(ts)[:, None]
    kj = np.arange(2 * ts)[None, :]
    dist = qi + ts - kj
    valid = (dist >= 0) & (dist < WINDOW)
    slopes = np.exp2(-8.0 * np.arange(1, N_Q_HEADS + 1, dtype=np.float32) / N_Q_HEADS).astype(np.float32)
    bias = np.where(valid[None], -slopes[:, None, None] * dist.astype(np.float32)[None], np.float32(NEG))
    bias = bias.astype(np.float32).reshape(N_Q_HEADS // 2, 2, ts, 2 * ts)
    return np.concatenate([bias[:, 0], bias[:, 1]], axis=-1)


def _mixers(proj, conv_w, conv_b, ln_g, ln_b, sconv_w, sinks, pool_w, pool_scale, gain, *, d):
    b, s, n = proj.shape
    ts = WINDOW
    ch = d // 4
    bias = jnp.asarray(_attn_bias_pairs(ts))
    full = lambda shape: pl.BlockSpec(shape, lambda bi, i: (0,) * len(shape))
    return pl.pallas_call(
        functools.partial(_mixer_kernel, ts=ts, ch=ch),
        grid=(b, s // ts),
        in_specs=[
            pl.BlockSpec((1, ts, n), lambda bi, i: (bi, i, 0)),
            pl.BlockSpec((1, ts, n), lambda bi, i: (bi, jnp.maximum(i - 1, 0), 0)),
            full((CONV_WIDTH, ch)), full((1, ch)), full((1, ch)), full((1, ch)),
            full((SCONV_WIDTH, ch)),
            pl.BlockSpec(memory_space=pltpu.SMEM),
            full(bias.shape),
            full(pool_w.shape), full((1, ch)), full((1, d)),
        ],
        out_specs=pl.BlockSpec((1, ts, d), lambda bi, i: (bi, i, 0)),
        out_shape=jax.ShapeDtypeStruct((b, s, d), BF16),
        scratch_shapes=[pltpu.VMEM((HALO + ts + SUBLANES, ch), F32), pltpu.VMEM((ts, ch), F32),
                        pltpu.VMEM((2, 4 * ts, LANES), BF16), pltpu.VMEM((2, 4 * ts, LANES), BF16)],
        compiler_params=_params(("parallel", "arbitrary")),
        name="mixers",
    )(proj, proj, conv_w, conv_b, ln_g, ln_b, sconv_w, sinks, bias, pool_w, pool_scale, gain)


def _outproj_kernel(y_ref, w_ref, h_ref, o_ref):
    o_ref[...] = h_ref[...] + jnp.dot(y_ref[...], w_ref[...], preferred_element_type=F32)


def _outproj(y, w, h, *, tm, tn):
    t, d = h.shape
    k = y.shape[1]
    return pl.pallas_call(
        _outproj_kernel,
        grid=(t // tm, d // tn),
        in_specs=[
            pl.BlockSpec((tm, k), lambda i, j: (i, 0)),
            pl.BlockSpec((k, tn), lambda i, j: (0, j)),
            pl.BlockSpec((tm, tn), lambda i, j: (i, j)),
        ],
        out_specs=pl.BlockSpec((tm, tn), lambda i, j: (i, j)),
        out_shape=jax.ShapeDtypeStruct((t, d), F32),
        compiler_params=_params(("parallel", "arbitrary")),
        name="outproj",
    )(y, w, h)


def _mlp_kernel(h_ref, g_ref, wu_ref, wd_ref, o_ref, hn_ref, up_ref, *, tm):
    j = pl.program_id(1)
    slot = j % 2

    @pl.when(j == 0)
    def _():
        _rmsnorm_rows(h_ref, g_ref, hn_ref, tm, copy_ref=o_ref)
        up_ref[1] = jnp.zeros(up_ref.shape[1:], up_ref.dtype)

    o_ref[...] += jnp.dot(up_ref[1 - slot], wd_ref[...], preferred_element_type=F32)
    up = jnp.dot(hn_ref[...], wu_ref[...], preferred_element_type=F32)
    up_ref[slot] = jnp.square(jnp.maximum(up, 0.0)).astype(up_ref.dtype)


def _mlp(h, g, wu, wd, *, tm, tf):
    t, d = h.shape
    nf = wu.shape[1] // tf
    return pl.pallas_call(
        functools.partial(_mlp_kernel, tm=tm),
        grid=(t // tm, nf + 1),
        in_specs=[
            pl.BlockSpec((tm, d), lambda i, j: (i, 0)),
            pl.BlockSpec((BF16_ROWS, d), lambda i, j: (0, 0)),
            pl.BlockSpec((d, tf), lambda i, j: (0, jnp.minimum(j, nf - 1))),
            pl.BlockSpec((tf, d), lambda i, j: (jnp.maximum(j - 1, 0), 0)),
        ],
        out_specs=pl.BlockSpec((tm, d), lambda i, j: (i, 0)),
        out_shape=jax.ShapeDtypeStruct((t, d), F32),
        scratch_shapes=[pltpu.VMEM((tm, d), BF16), pltpu.VMEM((2, tm, tf), BF16)],
        compiler_params=_params(("parallel", "arbitrary")),
        name="mlp",
    )(h, g, wu, wd)


def _ple_kernel(h_ref, hcol_ref, p_ref, wg_ref, wp_ref, o_ref, hb_ref, *, tm):
    @pl.when(pl.program_id(1) == 0)
    def _():
        def body(r, carry):
            rows = pl.ds(pl.multiple_of(r * BF16_ROWS, BF16_ROWS), BF16_ROWS)
            hb_ref[rows, :] = h_ref[rows, :].astype(BF16)
            return carry

        lax.fori_loop(0, tm // BF16_ROWS, body, 0)

    gate = jax.nn.sigmoid(jnp.dot(hb_ref[...], wg_ref[...], preferred_element_type=F32))
    emb = jnp.dot(p_ref[...].astype(BF16), wp_ref[...], preferred_element_type=F32)
    o_ref[...] = hcol_ref[...] + gate * emb


def _ple(h, p, wg, wp, *, tm, tn):
    t, d = h.shape
    kp = p.shape[1]
    return pl.pallas_call(
        functools.partial(_ple_kernel, tm=tm),
        grid=(t // tm, d // tn),
        in_specs=[
            pl.BlockSpec((tm, d), lambda i, j: (i, 0)),
            pl.BlockSpec((tm, tn), lambda i, j: (i, j)),
            pl.BlockSpec((tm, kp), lambda i, j: (i, 0)),
            pl.BlockSpec((d, tn), lambda i, j: (0, j)),
            pl.BlockSpec((kp, tn), lambda i, j: (0, j)),
        ],
        out_specs=pl.BlockSpec((tm, tn), lambda i, j: (i, j)),
        out_shape=jax.ShapeDtypeStruct((t, d), F32),
        scratch_shapes=[pltpu.VMEM((tm, d), BF16)],
        compiler_params=_params(("parallel", "arbitrary")),
        name="ple",
    )(h, h, p, wg, wp)


def _final_norm_kernel(h_ref, g_ref, o_ref, *, tm):
    _rmsnorm_rows(h_ref, g_ref, o_ref, tm)


def _final_norm(h, g, *, tm):
    t, d = h.shape
    return pl.pallas_call(
        functools.partial(_final_norm_kernel, tm=tm),
        grid=(t // tm,),
        in_specs=[pl.BlockSpec((tm, d), lambda i: (i, 0)), pl.BlockSpec((BF16_ROWS, d), lambda i: (0, 0))],
        out_specs=pl.BlockSpec((tm, d), lambda i: (i, 0)),
        out_shape=jax.ShapeDtypeStruct((t, d), F32),
        compiler_params=_params(("parallel",)),
        name="final_norm",
    )(h, g)


def kernel(x, p, norm_mix, w_in, conv_w, conv_b, conv_ln_g, conv_ln_b, sconv_w, attn_sinks, pool_w,
           pool_scale, mix_gain, w_out, norm_mlp, w_up, w_down, w_ple_gate, w_ple_proj, norm_final):
    b, s, d = x.shape
    depth = w_in.shape[0]
    t = b * s
    in_cols = w_in.shape[2]
    tn_in = 768
    in_cols_padded = -(-in_cols // tn_in) * tn_in
    row = lambda v: v.reshape(1, -1).astype(F32)
    rows16 = lambda v: jnp.broadcast_to(v.reshape(1, -1).astype(F32), (BF16_ROWS, v.shape[-1]))

    tm_small, tm_big = min(512, t), min(1024, t)
    h = x.reshape(t, d)
    for i in range(depth):
        proj = _inproj(h, rows16(norm_mix[i]), _cast_layer_bf16(w_in, i, in_cols_padded),
                       tm=tm_small, tn=tn_in)
        y = _mixers(proj.reshape(b, s, in_cols_padded), conv_w[i], row(conv_b[i]), row(conv_ln_g[i]),
                    row(conv_ln_b[i]), sconv_w[i], attn_sinks[i], pool_w[i].astype(BF16),
                    row(pool_scale[i]), row(mix_gain[i]), d=d)
        h = _outproj(y.reshape(t, d), _cast_layer_bf16(w_out, i), h, tm=tm_big, tn=512)
        h = _mlp(h, rows16(norm_mlp[i]), _cast_layer_bf16(w_up, i), _cast_layer_bf16(w_down, i),
                 tm=tm_small, tf=256)
        h = _ple(h, p[i].reshape(t, -1), _cast_layer_bf16(w_ple_gate, i), _cast_layer_bf16(w_ple_proj, i),
                 tm=tm_small, tn=512)
    out = _final_norm(h, rows16(norm_final), tm=256)
    return out.reshape(b, s, d)
```

```python
import functools

import numpy as np
import jax
import jax.numpy as jnp
from jax import lax
from jax.experimental import pallas as pl
from jax.experimental.pallas import tpu as pltpu

F32 = jnp.float32
BF16 = jnp.bfloat16
EPS = 1e-6

LANES = 128
BF16_ROWS = 16
SUBLANES = 8
VMEM_LIMIT_BYTES = 56 * 1024 * 1024
MLP_VMEM_LIMIT_BYTES = 60 * 1024 * 1024
CAST_BLOCK_BYTES = 4 * 1024 * 1024

CONV_WIDTH = 31
SCONV_WIDTH = 3
HEAD_DIM = 64
N_Q_HEADS = 16
WINDOW = 128
POOL_WINDOWS = (2, 4, 8, 16)
HALO = 32
NEG = -1e30


def _params(sem, vmem_limit_bytes=VMEM_LIMIT_BYTES):
    return pltpu.CompilerParams(dimension_semantics=sem, vmem_limit_bytes=vmem_limit_bytes)


def _rmsnorm_rows(src_ref, g_ref, dst_ref, n_rows, copy_ref=None):
    d = src_ref.shape[1]
    lane_chunks = [slice(c, c + LANES) for c in range(0, d, LANES)]

    def body(r, carry):
        rows = pl.ds(pl.multiple_of(r * BF16_ROWS, BF16_ROWS), BF16_ROWS)
        ss = [jnp.zeros((BF16_ROWS, LANES), F32) for _ in range(4)]
        for n, lanes in enumerate(lane_chunks):
            x = src_ref[rows, lanes]
            ss[n % 4] = ss[n % 4] + x * x
            if copy_ref is not None:
                copy_ref[rows, lanes] = x
        tot = jnp.sum((ss[0] + ss[1]) + (ss[2] + ss[3]), axis=-1, keepdims=True)
        scale = jnp.broadcast_to(lax.rsqrt(tot * (1.0 / d) + EPS), (BF16_ROWS, LANES))
        for lanes in lane_chunks:
            dst_ref[rows, lanes] = (src_ref[rows, lanes] * scale * g_ref[:, lanes]).astype(dst_ref.dtype)
        return carry

    lax.fori_loop(0, n_rows // BF16_ROWS, body, 0, unroll=2)


def _cast_kernel(w_ref, o_ref, *, n):
    o_ref[:, 0:n] = w_ref[...].astype(o_ref.dtype)
    if o_ref.shape[1] > n:
        o_ref[:, n:] = jnp.zeros((o_ref.shape[0], o_ref.shape[1] - n), o_ref.dtype)


def _cast_layer_bf16(w, layer, n_padded=None):
    _, k, n = w.shape
    n_padded = n if n_padded is None else n_padded
    rows_fit = CAST_BLOCK_BYTES // (4 * n)
    bk = min(k, max(BF16_ROWS, 1 << (rows_fit.bit_length() - 1)))
    return pl.pallas_call(
        functools.partial(_cast_kernel, n=n),
        grid=(k // bk,),
        in_specs=[pl.BlockSpec((None, bk, n), lambda r: (layer, r, 0))],
        out_specs=pl.BlockSpec((bk, n_padded), lambda r: (r, 0)),
        out_shape=jax.ShapeDtypeStruct((k, n_padded), BF16),
        compiler_params=_params(("parallel",)),
        name="cast_bf16",
    )(w)


def _inproj_kernel(h_ref, g_ref, w_ref, o_ref, hn_ref, *, tm):
    @pl.when(pl.program_id(1) == 0)
    def _():
        _rmsnorm_rows(h_ref, g_ref, hn_ref, tm)

    o_ref[...] = jnp.dot(hn_ref[...], w_ref[...], preferred_element_type=F32).astype(o_ref.dtype)


def _inproj(h, g, w, *, tm, tn):
    t, d = h.shape
    n = w.shape[1]
    return pl.pallas_call(
        functools.partial(_inproj_kernel, tm=tm),
        grid=(t // tm, n // tn),
        in_specs=[
            pl.BlockSpec((tm, d), lambda i, j: (i, 0)),
            pl.BlockSpec((BF16_ROWS, d), lambda i, j: (0, 0)),
            pl.BlockSpec((d, tn), lambda i, j: (0, j)),
        ],
        out_specs=pl.BlockSpec((tm, tn), lambda i, j: (i, j)),
        out_shape=jax.ShapeDtypeStruct((t, n), BF16),
        scratch_shapes=[pltpu.VMEM((tm, d), BF16)],
        compiler_params=_params(("parallel", "arbitrary")),
        name="inproj",
    )(h, g, w)


def _mixer_kernel(cur_ref, prev_ref, conv_w_ref, conv_b_ref, ln_g_ref, ln_b_ref, sconv_w_ref,
                  sinks_ref, bias_ref, pool_w_ref, pool_scale_ref, gain_ref, y_ref,
                  ext_ref, tmp_ref, kd_ref, vd_ref, *, ts, ch):
    i = pl.program_id(1)
    not_first = jnp.where(i > 0, 1.0, 0.0).astype(F32)
    n_lane_chunks = ch // LANES
    col_a, col_b, col_q = 0, 2 * ch, 5 * ch
    col_k = 6 * ch
    col_v = col_k + LANES
    col_d = col_v + LANES
    hrows = slice(ts - HALO, ts)

    def cur(c0, n=LANES):
        return cur_ref[0, :, c0:c0 + n]

    def prev_halo(c0, n=LANES):
        return prev_ref[0, hrows, c0:c0 + n].astype(F32)

    def finish(x_rows, col0, rows):
        ms = jnp.mean(x_rows * x_rows, axis=-1, keepdims=True)
        g = gain_ref[:, col0:col0 + ch]
        y_ref[0, rows, col0:col0 + ch] = (x_rows * lax.rsqrt(ms + EPS) * g).astype(y_ref.dtype)

    sub = 32
    row_chunks = [slice(r, r + sub) for r in range(0, ts, sub)]

    for c in range(n_lane_chunks):
        l0 = c * LANES
        val = cur(col_a + l0).astype(F32)
        gate = cur(col_a + ch + l0).astype(F32)
        ext_ref[HALO:HALO + ts, l0:l0 + LANES] = val * jax.nn.sigmoid(gate)
        pval = prev_halo(col_a + l0)
        pgate = prev_halo(col_a + ch + l0)
        ext_ref[0:HALO, l0:l0 + LANES] = pval * jax.nn.sigmoid(pgate) * not_first
    first = HALO - (CONV_WIDTH - 1)
    ext_ref[HALO + ts:HALO + ts + SUBLANES, :] = jnp.zeros((SUBLANES, ch), F32)
    for c in range(n_lane_chunks):
        l0 = c * LANES
        acc = conv_b_ref[:, l0:l0 + LANES]
        for b in range(SUBLANES):
            q = None
            for a in range(HALO // SUBLANES + 1):
                o = SUBLANES * a + b
                if first <= o <= HALO:
                    term = (ext_ref[SUBLANES * a:SUBLANES * a + ts + SUBLANES, l0:l0 + LANES]
                            * conv_w_ref[o - first:o - first + 1, l0:l0 + LANES])
                    q = term if q is None else q + term
            acc = acc + q[b:b + ts]
        tmp_ref[:, l0:l0 + LANES] = acc
    for rows in row_chunks:
        x = tmp_ref[rows, :]
        mu = jnp.mean(x, axis=-1, keepdims=True)
        xc = x - mu
        var = jnp.mean(xc * xc, axis=-1, keepdims=True)
        ln = xc * lax.rsqrt(var + EPS) * ln_g_ref[...] + ln_b_ref[...]
        finish(ln * jax.nn.sigmoid(ln), 0, rows)

    for c in range(n_lane_chunks):
        l0 = c * LANES
        cg = cur(col_b + ch + l0).astype(F32)
        xin = cur(col_b + 2 * ch + l0).astype(F32)
        ext_ref[HALO:HALO + ts, l0:l0 + LANES] = cg * xin
        ext_ref[0:HALO, l0:l0 + LANES] = (prev_halo(col_b + ch + l0) * prev_halo(col_b + 2 * ch + l0)
                                          * not_first)
    for c in range(n_lane_chunks):
        l0 = c * LANES
        acc = jnp.zeros((ts, LANES), F32)
        for k in range(SCONV_WIDTH):
            off = HALO - (SCONV_WIDTH - 1) + k
            acc = acc + ext_ref[off:off + ts, l0:l0 + LANES] * sconv_w_ref[k:k + 1, l0:l0 + LANES]
        tmp_ref[:, l0:l0 + LANES] = cur(col_b + l0).astype(F32) * acc
    for rows in row_chunks:
        finish(tmp_ref[rows, :], ch, rows)

    lane = lax.broadcasted_iota(jnp.int32, (2 * ts, LANES), 1)
    lo = lane < HEAD_DIM

    def block_diag(col0, dst_ref):
        both = jnp.concatenate([prev_ref[0, :, col0:col0 + LANES], cur(col0)], axis=0).astype(F32)
        swapped = pltpu.roll(both, HEAD_DIM, 1)
        zero = jnp.zeros_like(both)
        dst_ref[0, 0:2 * ts, :] = jnp.where(lo, both, zero).astype(BF16)
        dst_ref[0, 2 * ts:4 * ts, :] = jnp.where(lo, zero, swapped).astype(BF16)
        dst_ref[1, 0:2 * ts, :] = jnp.where(lo, swapped, zero).astype(BF16)
        dst_ref[1, 2 * ts:4 * ts, :] = jnp.where(lo, zero, both).astype(BF16)

    block_diag(col_k, kd_ref)
    block_diag(col_v, vd_ref)
    kcol = lax.broadcasted_iota(jnp.int32, (ts, 4 * ts), 1)
    before_start = (kcol & (2 * ts - 1)) < jnp.where(i == 0, ts, 0)
    lo_q = lax.broadcasted_iota(jnp.int32, (ts, LANES), 1) < HEAD_DIM
    heads_per_kv = N_Q_HEADS // 2
    for pair in range(N_Q_HEADS // 2):
        kvh = (2 * pair) // heads_per_kv
        q2 = cur(col_q + pair * LANES)
        s = lax.dot_general(q2, kd_ref[kvh], (((1,), (1,)), ((), ())), preferred_element_type=F32)
        s = s * (1.0 / np.sqrt(HEAD_DIM)).astype(np.float32) + bias_ref[pair]
        s = jnp.where(before_start, NEG, s)
        probs, inv = [], []
        for hh in range(2):
            sh = s[:, hh * 2 * ts:(hh + 1) * 2 * ts]
            sink = sinks_ref[2 * pair + hh]
            m = jnp.maximum(jnp.max(sh, axis=-1, keepdims=True), sink)
            e = jnp.exp(sh - m)
            den = jnp.sum(e, axis=-1, keepdims=True) + jnp.exp(sink - m)
            probs.append(e.astype(BF16))
            inv.append(1.0 / den)
        o = jnp.dot(jnp.concatenate(probs, axis=1), vd_ref[kvh], preferred_element_type=F32)
        tmp_ref[:, pair * LANES:(pair + 1) * LANES] = o * jnp.where(lo_q, inv[0], inv[1])
    for rows in row_chunks:
        finish(tmp_ref[rows, :], 2 * ch, rows)

    grp = ch // len(POOL_WINDOWS)
    t_glob = i * ts + lax.broadcasted_iota(jnp.int32, (ts, 1), 0)
    for gi, w in enumerate(POOL_WINDOWS):
        l0 = gi * grp
        d_cur = cur(col_d + l0, grp).astype(F32)
        ext_ref[HALO:HALO + ts, l0:l0 + grp] = d_cur
        ext_ref[0:HALO, l0:l0 + grp] = prev_halo(col_d + l0, grp) * not_first
        shift = 1
        while shift < w:
            hi = ext_ref[8:HALO + ts, l0:l0 + grp]
            sh_ = ext_ref[8 - shift:HALO + ts - shift, l0:l0 + grp]
            ext_ref[8:HALO + ts, l0:l0 + grp] = hi + sh_
            shift *= 2
        count = jnp.minimum(t_glob + 1, w).astype(F32)
        z = ext_ref[HALO:HALO + ts, l0:l0 + grp] / count - d_cur
        zz = jnp.dot(z.astype(BF16), pool_w_ref[gi], preferred_element_type=F32)
        tmp_ref[:, l0:l0 + grp] = zz * pool_scale_ref[:, l0:l0 + grp]
    for rows in row_chunks:
        finish(tmp_ref[rows, :], 3 * ch, rows)


def _attn_bias_pairs(ts):
    qi = np.arange(ts)[:, None]
    kj = np.arange(2 * ts)[None, :]
    dist = qi + ts - kj
    valid = (dist >= 0) & (dist < WINDOW)
    slopes = np.exp2(-8.0 * np.arange(1, N_Q_HEADS + 1, dtype=np.float32) / N_Q_HEADS).astype(np.float32)
    bias = np.where(valid[None], -slopes[:, None, None] * dist.astype(np.float32)[None], np.float32(NEG))
    bias = bias.astype(np.float32).reshape(N_Q_HEADS // 2, 2, ts, 2 * ts)
    return np.concatenate([bias[:, 0], bias[:, 1]], axis=-1)


def _mixers(proj, conv_w, conv_b, ln_g, ln_b, sconv_w, sinks, pool_w, pool_scale, gain, *, d):
    b, s, n = proj.shape
    ts = WINDOW
    ch = d // 4
    bias = jnp.asarray(_attn_bias_pairs(ts))
    full = lambda shape: pl.BlockSpec(shape, lambda bi, i: (0,) * len(shape))
    return pl.pallas_call(
        functools.partial(_mixer_kernel, ts=ts, ch=ch),
        grid=(b, s // ts),
        in_specs=[
            pl.BlockSpec((1, ts, n), lambda bi, i: (bi, i, 0)),
            pl.BlockSpec((1, ts, n), lambda bi, i: (bi, jnp.maximum(i - 1, 0), 0)),
            full((CONV_WIDTH, ch)), full((1, ch)), full((1, ch)), full((1, ch)),
            full((SCONV_WIDTH, ch)),
            pl.BlockSpec(memory_space=pltpu.SMEM),
            full(bias.shape),
            full(pool_w.shape), full((1, ch)), full((1, d)),
        ],
        out_specs=pl.BlockSpec((1, ts, d), lambda bi, i: (bi, i, 0)),
        out_shape=jax.ShapeDtypeStruct((b, s, d), BF16),
        scratch_shapes=[pltpu.VMEM((HALO + ts + SUBLANES, ch), F32), pltpu.VMEM((ts, ch), F32),
                        pltpu.VMEM((2, 4 * ts, LANES), BF16), pltpu.VMEM((2, 4 * ts, LANES), BF16)],
        compiler_params=_params(("parallel", "arbitrary")),
        name="mixers",
    )(proj, proj, conv_w, conv_b, ln_g, ln_b, sconv_w, sinks, bias, pool_w, pool_scale, gain)


def _outproj_kernel(y_ref, w_ref, h_ref, o_ref):
    o_ref[...] = h_ref[...] + jnp.dot(y_ref[...], w_ref[...], preferred_element_type=F32)


def _outproj(y, w, h, *, tm, tn):
    t, d = h.shape
    k = y.shape[1]
    return pl.pallas_call(
        _outproj_kernel,
        grid=(t // tm, d // tn),
        in_specs=[
            pl.BlockSpec((tm, k), lambda i, j: (i, 0)),
            pl.BlockSpec((k, tn), lambda i, j: (0, j)),
            pl.BlockSpec((tm, tn), lambda i, j: (i, j)),
        ],
        out_specs=pl.BlockSpec((tm, tn), lambda i, j: (i, j)),
        out_shape=jax.ShapeDtypeStruct((t, d), F32),
        compiler_params=_params(("parallel", "arbitrary")),
        name="outproj",
    )(y, w, h)


def _mlp_kernel(h_ref, g_ref, wu_ref, wd_ref, o_ref, hn_ref, up_ref, *, tm, nf):
    s = pl.program_id(0)
    f = s % nf
    slot = s % 2

    @pl.when(s == 0)
    def _():
        up_ref[1] = jnp.zeros(up_ref.shape[1:], up_ref.dtype)
        o_ref[...] = jnp.zeros(o_ref.shape, o_ref.dtype)

    @pl.when(f == 0)
    def _():
        _rmsnorm_rows(h_ref, g_ref, hn_ref, tm)

    @pl.when(f == 1)
    def _():
        o_ref[...] = h_ref[...]

    o_ref[...] += jnp.dot(up_ref[1 - slot], wd_ref[...], preferred_element_type=F32)
    up = jnp.dot(hn_ref[...], wu_ref[...], preferred_element_type=F32)
    up_ref[slot] = jnp.square(jnp.maximum(up, 0.0)).astype(up_ref.dtype)


def _mlp(h, g, wu, wd, *, tm, tf):
    t, d = h.shape
    nf = wu.shape[1] // tf
    nt = t // tm
    assert nf >= 2
    prev = lambda s: jnp.maximum(s - 1, 0)
    return pl.pallas_call(
        functools.partial(_mlp_kernel, tm=tm, nf=nf),
        grid=(nt * nf + 1,),
        in_specs=[
            pl.BlockSpec((tm, d), lambda s: (jnp.minimum(s // nf, nt - 1), 0)),
            pl.BlockSpec((BF16_ROWS, d), lambda s: (0, 0)),
            pl.BlockSpec((d, tf), lambda s: (0, s % nf)),
            pl.BlockSpec((tf, d), lambda s: (prev(s) % nf, 0)),
        ],
        out_specs=pl.BlockSpec((tm, d), lambda s: (prev(s) // nf, 0)),
        out_shape=jax.ShapeDtypeStruct((t, d), F32),
        scratch_shapes=[pltpu.VMEM((tm, d), BF16), pltpu.VMEM((2, tm, tf), BF16)],
        compiler_params=_params(("arbitrary",), MLP_VMEM_LIMIT_BYTES),
        name="mlp",
    )(h, g, wu, wd)


def _ple_kernel(h_ref, hcol_ref, p_ref, wg_ref, wp_ref, o_ref, hb_ref, z_ref, *, tm, nj):
    s = pl.program_id(0)
    slot = s % 2

    @pl.when(s == 0)
    def _():
        z_ref[1] = jnp.zeros(z_ref.shape[1:], z_ref.dtype)

    @pl.when(s % nj == 0)
    def _():
        def body(r, carry):
            rows = pl.ds(pl.multiple_of(r * BF16_ROWS, BF16_ROWS), BF16_ROWS)
            hb_ref[rows, :] = h_ref[rows, :].astype(BF16)
            return carry

        lax.fori_loop(0, tm // BF16_ROWS, body, 0)

    emb = jnp.dot(p_ref[...].astype(BF16), wp_ref[...], preferred_element_type=F32)
    o_ref[...] = hcol_ref[...] + jax.nn.sigmoid(z_ref[1 - slot]) * emb
    z_ref[slot] = jnp.dot(hb_ref[...], wg_ref[...], preferred_element_type=F32)


def _ple(h, p, wg, wp, *, tm, tn):
    t, d = h.shape
    kp = p.shape[1]
    nt, nj = t // tm, d // tn
    prev = lambda s: jnp.maximum(s - 1, 0)
    return pl.pallas_call(
        functools.partial(_ple_kernel, tm=tm, nj=nj),
        grid=(nt * nj + 1,),
        in_specs=[
            pl.BlockSpec((tm, d), lambda s: (jnp.minimum(s // nj, nt - 1), 0)),
            pl.BlockSpec((tm, tn), lambda s: (prev(s) // nj, prev(s) % nj)),
            pl.BlockSpec((tm, kp), lambda s: (prev(s) // nj, 0)),
            pl.BlockSpec((d, tn), lambda s: (0, s % nj)),
            pl.BlockSpec((kp, tn), lambda s: (0, prev(s) % nj)),
        ],
        out_specs=pl.BlockSpec((tm, tn), lambda s: (prev(s) // nj, prev(s) % nj)),
        out_shape=jax.ShapeDtypeStruct((t, d), F32),
        scratch_shapes=[pltpu.VMEM((tm, d), BF16), pltpu.VMEM((2, tm, tn), F32)],
        compiler_params=_params(("arbitrary",)),
        name="ple",
    )(h, h, p, wg, wp)


def _final_norm_kernel(h_ref, g_ref, o_ref, *, tm):
    _rmsnorm_rows(h_ref, g_ref, o_ref, tm)


def _final_norm(h, g, *, tm):
    t, d = h.shape
    return pl.pallas_call(
        functools.partial(_final_norm_kernel, tm=tm),
        grid=(t // tm,),
        in_specs=[pl.BlockSpec((tm, d), lambda i: (i, 0)), pl.BlockSpec((BF16_ROWS, d), lambda i: (0, 0))],
        out_specs=pl.BlockSpec((tm, d), lambda i: (i, 0)),
        out_shape=jax.ShapeDtypeStruct((t, d), F32),
        compiler_params=_params(("parallel",)),
        name="final_norm",
    )(h, g)


def kernel(x, p, norm_mix, w_in, conv_w, conv_b, conv_ln_g, conv_ln_b, sconv_w, attn_sinks, pool_w,
           pool_scale, mix_gain, w_out, norm_mlp, w_up, w_down, w_ple_gate, w_ple_proj, norm_final):
    b, s, d = x.shape
    depth = w_in.shape[0]
    t = b * s
    in_cols = w_in.shape[2]
    tn_in = 768
    in_cols_padded = -(-in_cols // tn_in) * tn_in
    row = lambda v: v.reshape(1, -1).astype(F32)
    rows16 = lambda v: jnp.broadcast_to(v.reshape(1, -1).astype(F32), (BF16_ROWS, v.shape[-1]))

    tm_small, tm_big = min(512, t), min(1024, t)
    h = x.reshape(t, d)
    for i in range(depth):
        proj = _inproj(h, rows16(norm_mix[i]), _cast_layer_bf16(w_in, i, in_cols_padded),
                       tm=tm_small, tn=tn_in)
        y = _mixers(proj.reshape(b, s, in_cols_padded), conv_w[i], row(conv_b[i]), row(conv_ln_g[i]),
                    row(conv_ln_b[i]), sconv_w[i], attn_sinks[i], pool_w[i].astype(BF16),
                    row(pool_scale[i]), row(mix_gain[i]), d=d)
        h = _outproj(y.reshape(t, d), _cast_layer_bf16(w_out, i), h, tm=tm_big, tn=512)
        h = _mlp(h, rows16(norm_mlp[i]), _cast_layer_bf16(w_up, i), _cast_layer_bf16(w_down, i),
                 tm=tm_small, tf=512)
        h = _ple(h, p[i].reshape(t, -1), _cast_layer_bf16(w_ple_gate, i), _cast_layer_bf16(w_ple_proj, i),
                 tm=tm_small, tn=1024)
    out = _final_norm(h, rows16(norm_final), tm=256)
    return out.reshape(b, s, d)
```

```python
import functools

import numpy as np
import jax
import jax.numpy as jnp
from jax import lax
from jax.experimental import pallas as pl
from jax.experimental.pallas import tpu as pltpu

F32 = jnp.float32
BF16 = jnp.bfloat16
EPS = 1e-6

LANES = 128
BF16_ROWS = 16
SUBLANES = 8
VMEM_LIMIT_BYTES = 56 * 1024 * 1024
MLP_VMEM_LIMIT_BYTES = 60 * 1024 * 1024
CAST_BLOCK_BYTES = 4 * 1024 * 1024

CONV_WIDTH = 31
SCONV_WIDTH = 3
HEAD_DIM = 64
N_Q_HEADS = 16
WINDOW = 128
POOL_WINDOWS = (2, 4, 8, 16)
HALO = 32
NEG = -1e30


def _params(sem, vmem_limit_bytes=VMEM_LIMIT_BYTES):
    return pltpu.CompilerParams(dimension_semantics=sem, vmem_limit_bytes=vmem_limit_bytes)


def _rmsnorm_rows(src_ref, g_ref, dst_ref, n_rows, copy_ref=None):
    d = src_ref.shape[1]
    lane_chunks = [slice(c, c + LANES) for c in range(0, d, LANES)]

    def body(r, carry):
        rows = pl.ds(pl.multiple_of(r * BF16_ROWS, BF16_ROWS), BF16_ROWS)
        ss = [jnp.zeros((BF16_ROWS, LANES), F32) for _ in range(4)]
        for n, lanes in enumerate(lane_chunks):
            x = src_ref[rows, lanes]
            ss[n % 4] = ss[n % 4] + x * x
            if copy_ref is not None:
                copy_ref[rows, lanes] = x
        tot = jnp.sum((ss[0] + ss[1]) + (ss[2] + ss[3]), axis=-1, keepdims=True)
        scale = jnp.broadcast_to(lax.rsqrt(tot * (1.0 / d) + EPS), (BF16_ROWS, LANES))
        for lanes in lane_chunks:
            dst_ref[rows, lanes] = (src_ref[rows, lanes] * scale * g_ref[:, lanes]).astype(dst_ref.dtype)
        return carry

    lax.fori_loop(0, n_rows // BF16_ROWS, body, 0, unroll=2)


def _cast_kernel(w_ref, o_ref, *, n):
    o_ref[:, 0:n] = w_ref[...].astype(o_ref.dtype)
    if o_ref.shape[1] > n:
        o_ref[:, n:] = jnp.zeros((o_ref.shape[0], o_ref.shape[1] - n), o_ref.dtype)


def _cast_layer_bf16(w, layer, n_padded=None):
    _, k, n = w.shape
    n_padded = n if n_padded is None else n_padded
    rows_fit = CAST_BLOCK_BYTES // (4 * n)
    bk = min(k, max(BF16_ROWS, 1 << (rows_fit.bit_length() - 1)))
    return pl.pallas_call(
        functools.partial(_cast_kernel, n=n),
        grid=(k // bk,),
        in_specs=[pl.BlockSpec((None, bk, n), lambda r: (layer, r, 0))],
        out_specs=pl.BlockSpec((bk, n_padded), lambda r: (r, 0)),
        out_shape=jax.ShapeDtypeStruct((k, n_padded), BF16),
        compiler_params=_params(("parallel",)),
        name="cast_bf16",
    )(w)


def _inproj_kernel(h_ref, g_ref, w_ref, o_ref, hn_ref, *, tm):
    @pl.when(pl.program_id(1) == 0)
    def _():
        _rmsnorm_rows(h_ref, g_ref, hn_ref, tm)

    o_ref[...] = jnp.dot(hn_ref[...], w_ref[...], preferred_element_type=F32).astype(o_ref.dtype)


def _inproj(h, g, w, *, tm, tn):
    t, d = h.shape
    n = w.shape[1]
    return pl.pallas_call(
        functools.partial(_inproj_kernel, tm=tm),
        grid=(t // tm, n // tn),
        in_specs=[
            pl.BlockSpec((tm, d), lambda i, j: (i, 0)),
            pl.BlockSpec((BF16_ROWS, d), lambda i, j: (0, 0)),
            pl.BlockSpec((d, tn), lambda i, j: (0, j)),
        ],
        out_specs=pl.BlockSpec((tm, tn), lambda i, j: (i, j)),
        out_shape=jax.ShapeDtypeStruct((t, n), BF16),
        scratch_shapes=[pltpu.VMEM((tm, d), BF16)],
        compiler_params=_params(("parallel", "arbitrary")),
        name="inproj",
    )(h, g, w)


N_MIXER_INPUTS = 12


def _mixer_kernel(*refs, ts, ch, n_riders):
    (cur_ref, prev_ref, conv_w_ref, conv_b_ref, ln_g_ref, ln_b_ref, sconv_w_ref,
     sinks_ref, bias_ref, pool_w_ref, pool_scale_ref, gain_ref) = refs[:N_MIXER_INPUTS]
    rider_in = refs[N_MIXER_INPUTS:N_MIXER_INPUTS + n_riders]
    y_ref = refs[N_MIXER_INPUTS + n_riders]
    rider_out = refs[N_MIXER_INPUTS + n_riders + 1:N_MIXER_INPUTS + 2 * n_riders + 1]
    ext_ref, tmp_ref, kd_ref, vd_ref = refs[N_MIXER_INPUTS + 2 * n_riders + 1:]

    i = pl.program_id(1)
    not_first = jnp.where(i > 0, 1.0, 0.0).astype(F32)
    n_lane_chunks = ch // LANES
    col_a, col_b, col_q = 0, 2 * ch, 5 * ch
    col_k = 6 * ch
    col_v = col_k + LANES
    col_d = col_v + LANES
    hrows = slice(ts - HALO, ts)

    def cur(c0, n=LANES):
        return cur_ref[0, :, c0:c0 + n]

    def prev_halo(c0, n=LANES):
        return prev_ref[0, hrows, c0:c0 + n].astype(F32)

    def finish(x_rows, col0, rows):
        ms = jnp.mean(x_rows * x_rows, axis=-1, keepdims=True)
        g = gain_ref[:, col0:col0 + ch]
        y_ref[0, rows, col0:col0 + ch] = (x_rows * lax.rsqrt(ms + EPS) * g).astype(y_ref.dtype)

    sub = 32
    row_chunks = [slice(r, r + sub) for r in range(0, ts, sub)]

    for c in range(n_lane_chunks):
        l0 = c * LANES
        val = cur(col_a + l0).astype(F32)
        gate = cur(col_a + ch + l0).astype(F32)
        ext_ref[HALO:HALO + ts, l0:l0 + LANES] = val * jax.nn.sigmoid(gate)
        pval = prev_halo(col_a + l0)
        pgate = prev_halo(col_a + ch + l0)
        ext_ref[0:HALO, l0:l0 + LANES] = pval * jax.nn.sigmoid(pgate) * not_first
    first = HALO - (CONV_WIDTH - 1)
    ext_ref[HALO + ts:HALO + ts + SUBLANES, :] = jnp.zeros((SUBLANES, ch), F32)
    for c in range(n_lane_chunks):
        l0 = c * LANES
        acc = conv_b_ref[:, l0:l0 + LANES]
        for b in range(SUBLANES):
            q = None
            for a in range(HALO // SUBLANES + 1):
                o = SUBLANES * a + b
                if first <= o <= HALO:
                    term = (ext_ref[SUBLANES * a:SUBLANES * a + ts + SUBLANES, l0:l0 + LANES]
                            * conv_w_ref[o - first:o - first + 1, l0:l0 + LANES])
                    q = term if q is None else q + term
            acc = acc + q[b:b + ts]
        tmp_ref[:, l0:l0 + LANES] = acc
    for rows in row_chunks:
        x = tmp_ref[rows, :]
        mu = jnp.mean(x, axis=-1, keepdims=True)
        xc = x - mu
        var = jnp.mean(xc * xc, axis=-1, keepdims=True)
        ln = xc * lax.rsqrt(var + EPS) * ln_g_ref[...] + ln_b_ref[...]
        finish(ln * jax.nn.sigmoid(ln), 0, rows)

    for c in range(n_lane_chunks):
        l0 = c * LANES
        cg = cur(col_b + ch + l0).astype(F32)
        xin = cur(col_b + 2 * ch + l0).astype(F32)
        ext_ref[HALO:HALO + ts, l0:l0 + LANES] = cg * xin
        ext_ref[0:HALO, l0:l0 + LANES] = (prev_halo(col_b + ch + l0) * prev_halo(col_b + 2 * ch + l0)
                                          * not_first)
    for c in range(n_lane_chunks):
        l0 = c * LANES
        acc = jnp.zeros((ts, LANES), F32)
        for k in range(SCONV_WIDTH):
            off = HALO - (SCONV_WIDTH - 1) + k
            acc = acc + ext_ref[off:off + ts, l0:l0 + LANES] * sconv_w_ref[k:k + 1, l0:l0 + LANES]
        tmp_ref[:, l0:l0 + LANES] = cur(col_b + l0).astype(F32) * acc
    for rows in row_chunks:
        finish(tmp_ref[rows, :], ch, rows)

    lane = lax.broadcasted_iota(jnp.int32, (2 * ts, LANES), 1)
    lo = lane < HEAD_DIM

    def block_diag(col0, dst_ref, scale):
        both = jnp.concatenate([prev_ref[0, :, col0:col0 + LANES], cur(col0)], axis=0).astype(F32)
        if scale != 1.0:
            both = both * scale
        swapped = pltpu.roll(both, HEAD_DIM, 1)
        zero = jnp.zeros_like(both)
        dst_ref[0, 0:2 * ts, :] = jnp.where(lo, both, zero).astype(BF16)
        dst_ref[0, 2 * ts:4 * ts, :] = jnp.where(lo, zero, swapped).astype(BF16)
        dst_ref[1, 0:2 * ts, :] = jnp.where(lo, swapped, zero).astype(BF16)
        dst_ref[1, 2 * ts:4 * ts, :] = jnp.where(lo, zero, both).astype(BF16)

    block_diag(col_k, kd_ref, 1.0 / np.sqrt(HEAD_DIM))
    block_diag(col_v, vd_ref, 1.0)
    bias_set = jnp.where(i == 0, 1, 0)
    lo_q = lax.broadcasted_iota(jnp.int32, (ts, LANES), 1) < HEAD_DIM
    heads_per_kv = N_Q_HEADS // 2
    for pair in range(N_Q_HEADS // 2):
        kvh = (2 * pair) // heads_per_kv
        q2 = cur(col_q + pair * LANES)
        s = lax.dot_general(q2, kd_ref[kvh], (((1,), (1,)), ((), ())), preferred_element_type=F32)
        s = s + bias_ref[bias_set, pair]
        probs, inv = [], []
        for hh in range(2):
            sh = s[:, hh * 2 * ts:(hh + 1) * 2 * ts]
            sink = sinks_ref[2 * pair + hh]
            m = jnp.maximum(jnp.max(sh, axis=-1, keepdims=True), sink)
            e = jnp.exp(sh - m)
            den = jnp.sum(e, axis=-1, keepdims=True) + jnp.exp(sink - m)
            probs.append(e.astype(BF16))
            inv.append(1.0 / den)
        o = jnp.dot(jnp.concatenate(probs, axis=1), vd_ref[kvh], preferred_element_type=F32)
        tmp_ref[:, pair * LANES:(pair + 1) * LANES] = o * jnp.where(lo_q, inv[0], inv[1])
    for rows in row_chunks:
        finish(tmp_ref[rows, :], 2 * ch, rows)

    grp = ch // len(POOL_WINDOWS)
    t_glob = i * ts + lax.broadcasted_iota(jnp.int32, (ts, 1), 0)
    for gi, w in enumerate(POOL_WINDOWS):
        l0 = gi * grp
        d_cur = cur(col_d + l0, grp).astype(F32)
        ext_ref[HALO:HALO + ts, l0:l0 + grp] = d_cur
        ext_ref[0:HALO, l0:l0 + grp] = prev_halo(col_d + l0, grp) * not_first
        shift = 1
        while shift < w:
            hi = ext_ref[8:HALO + ts, l0:l0 + grp]
            sh_ = ext_ref[8 - shift:HALO + ts - shift, l0:l0 + grp]
            ext_ref[8:HALO + ts, l0:l0 + grp] = hi + sh_
            shift *= 2
        count = jnp.minimum(t_glob + 1, w).astype(F32)
        z = ext_ref[HALO:HALO + ts, l0:l0 + grp] / count - d_cur
        zz = jnp.dot(z.astype(BF16), pool_w_ref[gi], preferred_element_type=F32)
        tmp_ref[:, l0:l0 + grp] = zz * pool_scale_ref[:, l0:l0 + grp]
    for rows in row_chunks:
        finish(tmp_ref[rows, :], 3 * ch, rows)

    for w_ref, wb_ref in zip(rider_in, rider_out):
        wb_ref[...] = w_ref[...].astype(wb_ref.dtype)


def _attn_bias_pairs(ts):
    qi = np.arange(ts)[:, None]
    kj = np.arange(2 * ts)[None, :]
    dist = qi + ts - kj
    valid = (dist >= 0) & (dist < WINDOW)
    slopes = np.exp2(-8.0 * np.arange(1, N_Q_HEADS + 1, dtype=np.float32) / N_Q_HEADS).astype(np.float32)
    alibi = -slopes[:, None, None] * dist.astype(np.float32)[None]
    sets = []
    for first_block in (False, True):
        ok = valid & ~((kj < ts) & first_block)
        bias = np.where(ok[None], alibi, np.float32(NEG)).astype(np.float32)
        bias = bias.reshape(N_Q_HEADS // 2, 2, ts, 2 * ts)
        sets.append(np.concatenate([bias[:, 0], bias[:, 1]], axis=-1))
    return np.stack(sets)


def _mixers(proj, conv_w, conv_b, ln_g, ln_b, sconv_w, sinks, pool_w, pool_scale, gain, *, d,
            layer, rider_weights):
    b, s, n = proj.shape
    ts = WINDOW
    ch = d // 4
    nblk = s // ts
    n_steps = b * nblk
    bias = jnp.asarray(_attn_bias_pairs(ts))
    full = lambda shape: pl.BlockSpec(shape, lambda bi, i: (0,) * len(shape))
    rider_in_specs, rider_out_specs, rider_out_shapes = [], [], []
    for w in rider_weights:
        _, k, wn = w.shape
        rb = k // n_steps
        assert rb * n_steps == k and rb % BF16_ROWS == 0
        rider_in_specs.append(pl.BlockSpec((None, rb, wn), lambda bi, i: (layer, bi * nblk + i, 0)))
        rider_out_specs.append(pl.BlockSpec((rb, wn), lambda bi, i: (bi * nblk + i, 0)))
        rider_out_shapes.append(jax.ShapeDtypeStruct((k, wn), BF16))
    outs = pl.pallas_call(
        functools.partial(_mixer_kernel, ts=ts, ch=ch, n_riders=len(rider_weights)),
        grid=(b, nblk),
        in_specs=[
            pl.BlockSpec((1, ts, n), lambda bi, i: (bi, i, 0)),
            pl.BlockSpec((1, ts, n), lambda bi, i: (bi, jnp.maximum(i - 1, 0), 0)),
            full((CONV_WIDTH, ch)), full((1, ch)), full((1, ch)), full((1, ch)),
            full((SCONV_WIDTH, ch)),
            pl.BlockSpec(memory_space=pltpu.SMEM),
            full(bias.shape),
            full(pool_w.shape), full((1, ch)), full((1, d)),
        ] + rider_in_specs,
        out_specs=[pl.BlockSpec((1, ts, d), lambda bi, i: (bi, i, 0))] + rider_out_specs,
        out_shape=[jax.ShapeDtypeStruct((b, s, d), BF16)] + rider_out_shapes,
        scratch_shapes=[pltpu.VMEM((HALO + ts + SUBLANES, ch), F32), pltpu.VMEM((ts, ch), F32),
                        pltpu.VMEM((2, 4 * ts, LANES), BF16), pltpu.VMEM((2, 4 * ts, LANES), BF16)],
        compiler_params=_params(("arbitrary", "arbitrary")),
        name="mixers",
    )(proj, proj, conv_w, conv_b, ln_g, ln_b, sconv_w, sinks, bias, pool_w, pool_scale, gain,
      *rider_weights)
    return outs[0], outs[1:]


def _outproj_kernel(y_ref, w_ref, h_ref, o_ref):
    o_ref[...] = h_ref[...] + jnp.dot(y_ref[...], w_ref[...], preferred_element_type=F32)


def _outproj(y, w, h, *, tm, tn):
    t, d = h.shape
    k = y.shape[1]
    return pl.pallas_call(
        _outproj_kernel,
        grid=(t // tm, d // tn),
        in_specs=[
            pl.BlockSpec((tm, k), lambda i, j: (i, 0)),
            pl.BlockSpec((k, tn), lambda i, j: (0, j)),
            pl.BlockSpec((tm, tn), lambda i, j: (i, j)),
        ],
        out_specs=pl.BlockSpec((tm, tn), lambda i, j: (i, j)),
        out_shape=jax.ShapeDtypeStruct((t, d), F32),
        compiler_params=_params(("parallel", "arbitrary")),
        name="outproj",
    )(y, w, h)


def _mlp_kernel(h_ref, g_ref, wu_ref, wd_ref, o_ref, hn_ref, up_ref, *, tm, nf):
    s = pl.program_id(0)
    f = s % nf
    slot = s % 2

    @pl.when(s == 0)
    def _():
        up_ref[1] = jnp.zeros(up_ref.shape[1:], up_ref.dtype)
        o_ref[...] = jnp.zeros(o_ref.shape, o_ref.dtype)

    @pl.when(f == 0)
    def _():
        _rmsnorm_rows(h_ref, g_ref, hn_ref, tm)

    @pl.when(f == 1)
    def _():
        o_ref[...] = h_ref[...]

    o_ref[...] += jnp.dot(up_ref[1 - slot], wd_ref[...], preferred_element_type=F32)
    up = jnp.dot(hn_ref[...], wu_ref[...], preferred_element_type=F32)
    up_ref[slot] = jnp.square(jnp.maximum(up, 0.0)).astype(up_ref.dtype)


def _mlp(h, g, wu, wd, *, tm, tf):
    t, d = h.shape
    nf = wu.shape[1] // tf
    nt = t // tm
    assert nf >= 2
    prev = lambda s: jnp.maximum(s - 1, 0)
    return pl.pallas_call(
        functools.partial(_mlp_kernel, tm=tm, nf=nf),
        grid=(nt * nf + 1,),
        in_specs=[
            pl.BlockSpec((tm, d), lambda s: (jnp.minimum(s // nf, nt - 1), 0)),
            pl.BlockSpec((BF16_ROWS, d), lambda s: (0, 0)),
            pl.BlockSpec((d, tf), lambda s: (0, s % nf)),
            pl.BlockSpec((tf, d), lambda s: (prev(s) % nf, 0)),
        ],
        out_specs=pl.BlockSpec((tm, d), lambda s: (prev(s) // nf, 0)),
        out_shape=jax.ShapeDtypeStruct((t, d), F32),
        scratch_shapes=[pltpu.VMEM((tm, d), BF16), pltpu.VMEM((2, tm, tf), BF16)],
        compiler_params=_params(("arbitrary",), MLP_VMEM_LIMIT_BYTES),
        name="mlp",
    )(h, g, wu, wd)


def _ple_kernel(h_ref, hcol_ref, p_ref, wg_ref, wp_ref, o_ref, hb_ref, z_ref, *, tm, nj):
    s = pl.program_id(0)
    slot = s % 2

    @pl.when(s == 0)
    def _():
        z_ref[1] = jnp.zeros(z_ref.shape[1:], z_ref.dtype)

    @pl.when(s % nj == 0)
    def _():
        def body(r, carry):
            rows = pl.ds(pl.multiple_of(r * BF16_ROWS, BF16_ROWS), BF16_ROWS)
            hb_ref[rows, :] = h_ref[rows, :].astype(BF16)
            return carry

        lax.fori_loop(0, tm // BF16_ROWS, body, 0)

    emb = jnp.dot(p_ref[...].astype(BF16), wp_ref[...], preferred_element_type=F32)
    o_ref[...] = hcol_ref[...] + jax.nn.sigmoid(z_ref[1 - slot]) * emb
    z_ref[slot] = jnp.dot(hb_ref[...], wg_ref[...], preferred_element_type=F32)


def _ple(h, p, wg, wp, *, tm, tn):
    t, d = h.shape
    kp = p.shape[1]
    nt, nj = t // tm, d // tn
    prev = lambda s: jnp.maximum(s - 1, 0)
    return pl.pallas_call(
        functools.partial(_ple_kernel, tm=tm, nj=nj),
        grid=(nt * nj + 1,),
        in_specs=[
            pl.BlockSpec((tm, d), lambda s: (jnp.minimum(s // nj, nt - 1), 0)),
            pl.BlockSpec((tm, tn), lambda s: (prev(s) // nj, prev(s) % nj)),
            pl.BlockSpec((tm, kp), lambda s: (prev(s) // nj, 0)),
            pl.BlockSpec((d, tn), lambda s: (0, s % nj)),
            pl.BlockSpec((kp, tn), lambda s: (0, prev(s) % nj)),
        ],
        out_specs=pl.BlockSpec((tm, tn), lambda s: (prev(s) // nj, prev(s) % nj)),
        out_shape=jax.ShapeDtypeStruct((t, d), F32),
        scratch_shapes=[pltpu.VMEM((tm, d), BF16), pltpu.VMEM((2, tm, tn), F32)],
        compiler_params=_params(("arbitrary",)),
        name="ple",
    )(h, h, p, wg, wp)


def _final_norm_kernel(h_ref, g_ref, o_ref, *, tm):
    _rmsnorm_rows(h_ref, g_ref, o_ref, tm)


def _final_norm(h, g, *, tm):
    t, d = h.shape
    return pl.pallas_call(
        functools.partial(_final_norm_kernel, tm=tm),
        grid=(t // tm,),
        in_specs=[pl.BlockSpec((tm, d), lambda i: (i, 0)), pl.BlockSpec((BF16_ROWS, d), lambda i: (0, 0))],
        out_specs=pl.BlockSpec((tm, d), lambda i: (i, 0)),
        out_shape=jax.ShapeDtypeStruct((t, d), F32),
        compiler_params=_params(("parallel",)),
        name="final_norm",
    )(h, g)


def kernel(x, p, norm_mix, w_in, conv_w, conv_b, conv_ln_g, conv_ln_b, sconv_w, attn_sinks, pool_w,
           pool_scale, mix_gain, w_out, norm_mlp, w_up, w_down, w_ple_gate, w_ple_proj, norm_final):
    b, s, d = x.shape
    depth = w_in.shape[0]
    t = b * s
    in_cols = w_in.shape[2]
    tn_in = 768
    in_cols_padded = -(-in_cols // tn_in) * tn_in
    row = lambda v: v.reshape(1, -1).astype(F32)
    rows16 = lambda v: jnp.broadcast_to(v.reshape(1, -1).astype(F32), (BF16_ROWS, v.shape[-1]))

    tm_small, tm_big = min(512, t), min(1024, t)
    h = x.reshape(t, d)
    for i in range(depth):
        proj = _inproj(h, rows16(norm_mix[i]), _cast_layer_bf16(w_in, i, in_cols_padded),
                       tm=tm_small, tn=tn_in)
        y, (w_out_b, w_up_b, w_down_b, w_gate_b) = _mixers(
            proj.reshape(b, s, in_cols_padded), conv_w[i], row(conv_b[i]), row(conv_ln_g[i]),
            row(conv_ln_b[i]), sconv_w[i], attn_sinks[i], pool_w[i].astype(BF16),
            row(pool_scale[i]), row(mix_gain[i]), d=d, layer=i,
            rider_weights=(w_out, w_up, w_down, w_ple_gate))
        h = _outproj(y.reshape(t, d), w_out_b, h, tm=tm_big, tn=512)
        h = _mlp(h, rows16(norm_mlp[i]), w_up_b, w_down_b, tm=tm_small, tf=512)
        h = _ple(h, p[i].reshape(t, -1), w_gate_b, _cast_layer_bf16(w_ple_proj, i),
                 tm=tm_small, tn=1024)
    out = _final_norm(h, rows16(norm_final), tm=256)
    return out.reshape(b, s, d)
```

```python
import functools

import numpy as np
import jax
import jax.numpy as jnp
from jax import lax
from jax.experimental import pallas as pl
from jax.experimental.pallas import tpu as pltpu

F32 = jnp.float32
BF16 = jnp.bfloat16
EPS = 1e-6

LANES = 128
BF16_ROWS = 16
SUBLANES = 8
VMEM_LIMIT_BYTES = 56 * 1024 * 1024
MLP_VMEM_LIMIT_BYTES = 60 * 1024 * 1024
CAST_BLOCK_BYTES = 4 * 1024 * 1024

CONV_WIDTH = 31
SCONV_WIDTH = 3
HEAD_DIM = 64
N_Q_HEADS = 16
WINDOW = 128
POOL_WINDOWS = (2, 4, 8, 16)
HALO = 32
NEG = -1e30


def _params(sem, vmem_limit_bytes=VMEM_LIMIT_BYTES):
    return pltpu.CompilerParams(dimension_semantics=sem, vmem_limit_bytes=vmem_limit_bytes)


def _rmsnorm_rows(src_ref, g_ref, dst_ref, n_rows, copy_ref=None):
    d = src_ref.shape[1]
    lane_chunks = [slice(c, c + LANES) for c in range(0, d, LANES)]

    def body(r, carry):
        rows = pl.ds(pl.multiple_of(r * BF16_ROWS, BF16_ROWS), BF16_ROWS)
        ss = [jnp.zeros((BF16_ROWS, LANES), F32) for _ in range(4)]
        for n, lanes in enumerate(lane_chunks):
            x = src_ref[rows, lanes]
            ss[n % 4] = ss[n % 4] + x * x
            if copy_ref is not None:
                copy_ref[rows, lanes] = x
        tot = jnp.sum((ss[0] + ss[1]) + (ss[2] + ss[3]), axis=-1, keepdims=True)
        scale = jnp.broadcast_to(lax.rsqrt(tot * (1.0 / d) + EPS), (BF16_ROWS, LANES))
        for lanes in lane_chunks:
            dst_ref[rows, lanes] = (src_ref[rows, lanes] * scale * g_ref[:, lanes]).astype(dst_ref.dtype)
        return carry

    lax.fori_loop(0, n_rows // BF16_ROWS, body, 0, unroll=8)


def _cast_kernel(w_ref, o_ref, *, n):
    o_ref[:, 0:n] = w_ref[...].astype(o_ref.dtype)
    if o_ref.shape[1] > n:
        o_ref[:, n:] = jnp.zeros((o_ref.shape[0], o_ref.shape[1] - n), o_ref.dtype)


def _cast_layer_bf16(w, layer, n_padded=None):
    _, k, n = w.shape
    n_padded = n if n_padded is None else n_padded
    rows_fit = CAST_BLOCK_BYTES // (4 * n)
    bk = min(k, max(BF16_ROWS, 1 << (rows_fit.bit_length() - 1)))
    return pl.pallas_call(
        functools.partial(_cast_kernel, n=n),
        grid=(k // bk,),
        in_specs=[pl.BlockSpec((None, bk, n), lambda r: (layer, r, 0))],
        out_specs=pl.BlockSpec((bk, n_padded), lambda r: (r, 0)),
        out_shape=jax.ShapeDtypeStruct((k, n_padded), BF16),
        compiler_params=_params(("parallel",)),
        name="cast_bf16",
    )(w)


def _inproj_kernel(h_ref, g_ref, w_ref, o_ref, hn_ref, *, tm):
    @pl.when(pl.program_id(1) == 0)
    def _():
        _rmsnorm_rows(h_ref, g_ref, hn_ref, tm)

    o_ref[...] = jnp.dot(hn_ref[...], w_ref[...], preferred_element_type=F32).astype(o_ref.dtype)


def _inproj(h, g, w, *, tm, tn):
    t, d = h.shape
    n = w.shape[1]
    return pl.pallas_call(
        functools.partial(_inproj_kernel, tm=tm),
        grid=(t // tm, n // tn),
        in_specs=[
            pl.BlockSpec((tm, d), lambda i, j: (i, 0)),
            pl.BlockSpec((BF16_ROWS, d), lambda i, j: (0, 0)),
            pl.BlockSpec((d, tn), lambda i, j: (0, j)),
        ],
        out_specs=pl.BlockSpec((tm, tn), lambda i, j: (i, j)),
        out_shape=jax.ShapeDtypeStruct((t, n), BF16),
        scratch_shapes=[pltpu.VMEM((tm, d), BF16)],
        compiler_params=_params(("parallel", "arbitrary")),
        name="inproj",
    )(h, g, w)


N_MIXER_INPUTS = 12


def _mixer_kernel(*refs, ts, ch, n_riders):
    (cur_ref, prev_ref, conv_w_ref, conv_b_ref, ln_g_ref, ln_b_ref, sconv_w_ref,
     sinks_ref, bias_ref, pool_w_ref, pool_scale_ref, gain_ref) = refs[:N_MIXER_INPUTS]
    rider_in = refs[N_MIXER_INPUTS:N_MIXER_INPUTS + n_riders]
    y_ref = refs[N_MIXER_INPUTS + n_riders]
    rider_out = refs[N_MIXER_INPUTS + n_riders + 1:N_MIXER_INPUTS + 2 * n_riders + 1]
    ext_ref, tmp_ref, kd_ref, vd_ref = refs[N_MIXER_INPUTS + 2 * n_riders + 1:]
    ext_a, ext_b, ext_d = ext_ref.at[0], ext_ref.at[1], ext_ref.at[2]
    tmp_a, tmp_b, tmp_c, tmp_d = tmp_ref.at[0], tmp_ref.at[1], tmp_ref.at[2], tmp_ref.at[3]

    i = pl.program_id(1)
    not_first = jnp.where(i > 0, 1.0, 0.0).astype(F32)
    n_lane_chunks = ch // LANES
    col_a, col_b, col_q = 0, 2 * ch, 5 * ch
    col_k = 6 * ch
    col_v = col_k + LANES
    col_d = col_v + LANES
    hrows = slice(ts - HALO, ts)

    def cur(c0, n=LANES):
        return cur_ref[0, :, c0:c0 + n]

    def prev_halo(c0, n=LANES):
        return prev_ref[0, hrows, c0:c0 + n].astype(F32)

    def finish(x_rows, col0, rows):
        ms = jnp.mean(x_rows * x_rows, axis=-1, keepdims=True)
        g = gain_ref[:, col0:col0 + ch]
        y_ref[0, rows, col0:col0 + ch] = (x_rows * lax.rsqrt(ms + EPS) * g).astype(y_ref.dtype)

    sub = 32
    row_chunks = [slice(r, r + sub) for r in range(0, ts, sub)]

    for c in range(n_lane_chunks):
        l0 = c * LANES
        val = cur(col_a + l0).astype(F32)
        gate = cur(col_a + ch + l0).astype(F32)
        ext_a[HALO:HALO + ts, l0:l0 + LANES] = val * jax.nn.sigmoid(gate)
        pval = prev_halo(col_a + l0)
        pgate = prev_halo(col_a + ch + l0)
        ext_a[0:HALO, l0:l0 + LANES] = pval * jax.nn.sigmoid(pgate) * not_first
    first = HALO - (CONV_WIDTH - 1)
    ext_a[HALO + ts:HALO + ts + SUBLANES, :] = jnp.zeros((SUBLANES, ch), F32)
    for c in range(n_lane_chunks):
        l0 = c * LANES
        acc = conv_b_ref[:, l0:l0 + LANES]
        for b in range(SUBLANES):
            q = None
            for a in range(HALO // SUBLANES + 1):
                o = SUBLANES * a + b
                if first <= o <= HALO:
                    term = (ext_a[SUBLANES * a:SUBLANES * a + ts + SUBLANES, l0:l0 + LANES]
                            * conv_w_ref[o - first:o - first + 1, l0:l0 + LANES])
                    q = term if q is None else q + term
            acc = acc + q[b:b + ts]
        tmp_a[:, l0:l0 + LANES] = acc
    for rows in row_chunks:
        x = tmp_a[rows, :]
        mu = jnp.mean(x, axis=-1, keepdims=True)
        xc = x - mu
        var = jnp.mean(xc * xc, axis=-1, keepdims=True)
        ln = xc * lax.rsqrt(var + EPS) * ln_g_ref[...] + ln_b_ref[...]
        finish(ln * jax.nn.sigmoid(ln), 0, rows)

    for c in range(n_lane_chunks):
        l0 = c * LANES
        cg = cur(col_b + ch + l0).astype(F32)
        xin = cur(col_b + 2 * ch + l0).astype(F32)
        ext_b[HALO:HALO + ts, l0:l0 + LANES] = cg * xin
        ext_b[0:HALO, l0:l0 + LANES] = (prev_halo(col_b + ch + l0) * prev_halo(col_b + 2 * ch + l0)
                                        * not_first)
    for c in range(n_lane_chunks):
        l0 = c * LANES
        acc = jnp.zeros((ts, LANES), F32)
        for k in range(SCONV_WIDTH):
            off = HALO - (SCONV_WIDTH - 1) + k
            acc = acc + ext_b[off:off + ts, l0:l0 + LANES] * sconv_w_ref[k:k + 1, l0:l0 + LANES]
        tmp_b[:, l0:l0 + LANES] = cur(col_b + l0).astype(F32) * acc
    for rows in row_chunks:
        finish(tmp_b[rows, :], ch, rows)

    lane = lax.broadcasted_iota(jnp.int32, (2 * ts, LANES), 1)
    lo = lane < HEAD_DIM

    def block_diag(col0, dst_ref, scale):
        both = jnp.concatenate([prev_ref[0, :, col0:col0 + LANES], cur(col0)], axis=0).astype(F32)
        if scale != 1.0:
            both = both * scale
        swapped = pltpu.roll(both, HEAD_DIM, 1)
        zero = jnp.zeros_like(both)
        dst_ref[0, 0:2 * ts, 0:LANES] = jnp.where(lo, both, zero).astype(BF16)
        dst_ref[0, 2 * ts:4 * ts, 0:LANES] = jnp.where(lo, zero, swapped).astype(BF16)
        dst_ref[1, 0:2 * ts, 0:LANES] = jnp.where(lo, swapped, zero).astype(BF16)
        dst_ref[1, 2 * ts:4 * ts, 0:LANES] = jnp.where(lo, zero, both).astype(BF16)

    block_diag(col_k, kd_ref, 1.0 / np.sqrt(HEAD_DIM))
    block_diag(col_v, vd_ref, 1.0)
    ones_a = jnp.where(lo, 1.0, 0.0).astype(BF16)
    ones_b = jnp.where(lo, 0.0, 1.0).astype(BF16)
    for kvh in range(2):
        vd_ref[kvh, 0:2 * ts, LANES:2 * LANES] = ones_a
        vd_ref[kvh, 2 * ts:4 * ts, LANES:2 * LANES] = ones_b
    bias_set = jnp.where(i == 0, 1, 0)
    lo_q = lax.broadcasted_iota(jnp.int32, (ts, LANES), 1) < HEAD_DIM
    heads_per_kv = N_Q_HEADS // 2
    for pair in range(N_Q_HEADS // 2):
        kvh = (2 * pair) // heads_per_kv
        q2 = cur(col_q + pair * LANES)
        s = lax.dot_general(q2, kd_ref[kvh], (((1,), (1,)), ((), ())), preferred_element_type=F32)
        s = s + bias_ref[bias_set, pair]
        probs, sink_e = [], []
        for hh in range(2):
            sh = s[:, hh * 2 * ts:(hh + 1) * 2 * ts]
            sink = sinks_ref[2 * pair + hh]
            m = jnp.maximum(jnp.max(sh, axis=-1, keepdims=True), sink)
            probs.append(jnp.exp(sh - m).astype(BF16))
            sink_e.append(jnp.exp(sink - m))
        o = jnp.dot(jnp.concatenate(probs, axis=1), vd_ref[kvh], preferred_element_type=F32)
        den = o[:, LANES:2 * LANES] + jnp.where(lo_q, sink_e[0], sink_e[1])
        tmp_c[:, pair * LANES:(pair + 1) * LANES] = o[:, 0:LANES] / den
    for rows in row_chunks:
        finish(tmp_c[rows, :], 2 * ch, rows)

    grp = ch // len(POOL_WINDOWS)
    t_glob = i * ts + lax.broadcasted_iota(jnp.int32, (ts, 1), 0)
    for gi, w in enumerate(POOL_WINDOWS):
        l0 = gi * grp
        d_cur = cur(col_d + l0, grp).astype(F32)
        ext_d[HALO:HALO + ts, l0:l0 + grp] = d_cur
        ext_d[0:HALO, l0:l0 + grp] = prev_halo(col_d + l0, grp) * not_first
        shift = 1
        while shift < w:
            hi = ext_d[8:HALO + ts, l0:l0 + grp]
            sh_ = ext_d[8 - shift:HALO + ts - shift, l0:l0 + grp]
            ext_d[8:HALO + ts, l0:l0 + grp] = hi + sh_
            shift *= 2
        count = jnp.minimum(t_glob + 1, w).astype(F32)
        z = ext_d[HALO:HALO + ts, l0:l0 + grp] / count - d_cur
        zz = jnp.dot(z.astype(BF16), pool_w_ref[gi], preferred_element_type=F32)
        tmp_d[:, l0:l0 + grp] = zz * pool_scale_ref[:, l0:l0 + grp]
    for rows in row_chunks:
        finish(tmp_d[rows, :], 3 * ch, rows)

    for w_ref, wb_ref in zip(rider_in, rider_out):
        wb_ref[...] = w_ref[...].astype(wb_ref.dtype)


def _attn_bias_pairs(ts):
    qi = np.arange(ts)[:, None]
    kj = np.arange(2 * ts)[None, :]
    dist = qi + ts - kj
    valid = (dist >= 0) & (dist < WINDOW)
    slopes = np.exp2(-8.0 * np.arange(1, N_Q_HEADS + 1, dtype=np.float32) / N_Q_HEADS).astype(np.float32)
    alibi = -slopes[:, None, None] * dist.astype(np.float32)[None]
    sets = []
    for first_block in (False, True):
        ok = valid & ~((kj < ts) & first_block)
        bias = np.where(ok[None], alibi, np.float32(NEG)).astype(np.float32)
        bias = bias.reshape(N_Q_HEADS // 2, 2, ts, 2 * ts)
        sets.append(np.concatenate([bias[:, 0], bias[:, 1]], axis=-1))
    return np.stack(sets)


def _mixers(proj, conv_w, conv_b, ln_g, ln_b, sconv_w, sinks, pool_w, pool_scale, gain, *, d,
            layer, rider_weights):
    b, s, n = proj.shape
    ts = WINDOW
    ch = d // 4
    nblk = s // ts
    n_steps = b * nblk
    bias = jnp.asarray(_attn_bias_pairs(ts))
    full = lambda shape: pl.BlockSpec(shape, lambda bi, i: (0,) * len(shape))
    rider_in_specs, rider_out_specs, rider_out_shapes = [], [], []
    for w in rider_weights:
        _, k, wn = w.shape
        rb = k // n_steps
        assert rb * n_steps == k and rb % BF16_ROWS == 0
        rider_in_specs.append(pl.BlockSpec((None, rb, wn), lambda bi, i: (layer, bi * nblk + i, 0)))
        rider_out_specs.append(pl.BlockSpec((rb, wn), lambda bi, i: (bi * nblk + i, 0)))
        rider_out_shapes.append(jax.ShapeDtypeStruct((k, wn), BF16))
    outs = pl.pallas_call(
        functools.partial(_mixer_kernel, ts=ts, ch=ch, n_riders=len(rider_weights)),
        grid=(b, nblk),
        in_specs=[
            pl.BlockSpec((1, ts, n), lambda bi, i: (bi, i, 0)),
            pl.BlockSpec((1, ts, n), lambda bi, i: (bi, jnp.maximum(i - 1, 0), 0)),
            full((CONV_WIDTH, ch)), full((1, ch)), full((1, ch)), full((1, ch)),
            full((SCONV_WIDTH, ch)),
            pl.BlockSpec(memory_space=pltpu.SMEM),
            full(bias.shape),
            full(pool_w.shape), full((1, ch)), full((1, d)),
        ] + rider_in_specs,
        out_specs=[pl.BlockSpec((1, ts, d), lambda bi, i: (bi, i, 0))] + rider_out_specs,
        out_shape=[jax.ShapeDtypeStruct((b, s, d), BF16)] + rider_out_shapes,
        scratch_shapes=[pltpu.VMEM((3, HALO + ts + SUBLANES, ch), F32), pltpu.VMEM((4, ts, ch), F32),
                        pltpu.VMEM((2, 4 * ts, LANES), BF16), pltpu.VMEM((2, 4 * ts, 2 * LANES), BF16)],
        compiler_params=_params(("arbitrary", "arbitrary")),
        name="mixers",
    )(proj, proj, conv_w, conv_b, ln_g, ln_b, sconv_w, sinks, bias, pool_w, pool_scale, gain,
      *rider_weights)
    return outs[0], outs[1:]


def _outproj_kernel(y_ref, w_ref, h_ref, o_ref):
    o_ref[...] = h_ref[...] + jnp.dot(y_ref[...], w_ref[...], preferred_element_type=F32)


def _outproj(y, w, h, *, tm, tn):
    t, d = h.shape
    k = y.shape[1]
    return pl.pallas_call(
        _outproj_kernel,
        grid=(t // tm, d // tn),
        in_specs=[
            pl.BlockSpec((tm, k), lambda i, j: (i, 0)),
            pl.BlockSpec((k, tn), lambda i, j: (0, j)),
            pl.BlockSpec((tm, tn), lambda i, j: (i, j)),
        ],
        out_specs=pl.BlockSpec((tm, tn), lambda i, j: (i, j)),
        out_shape=jax.ShapeDtypeStruct((t, d), F32),
        compiler_params=_params(("parallel", "arbitrary")),
        name="outproj",
    )(y, w, h)


def _mlp_kernel(h_ref, g_ref, wu_ref, wd_ref, o_ref, hn_ref, up_ref, *, tm, nf):
    s = pl.program_id(0)
    f = s % nf
    slot = s % 2

    @pl.when(s == 0)
    def _():
        up_ref[1] = jnp.zeros(up_ref.shape[1:], up_ref.dtype)
        o_ref[...] = jnp.zeros(o_ref.shape, o_ref.dtype)

    @pl.when(f == 0)
    def _():
        _rmsnorm_rows(h_ref, g_ref, hn_ref, tm)

    @pl.when(f == 1)
    def _():
        o_ref[...] = h_ref[...]

    o_ref[...] += jnp.dot(up_ref[1 - slot], wd_ref[...], preferred_element_type=F32)
    up = jnp.dot(hn_ref[...], wu_ref[...], preferred_element_type=F32)
    up_ref[slot] = jnp.square(jnp.maximum(up, 0.0)).astype(up_ref.dtype)


def _mlp(h, g, wu, wd, *, tm, tf):
    t, d = h.shape
    nf = wu.shape[1] // tf
    nt = t // tm
    assert nf >= 2
    prev = lambda s: jnp.maximum(s - 1, 0)
    return pl.pallas_call(
        functools.partial(_mlp_kernel, tm=tm, nf=nf),
        grid=(nt * nf + 1,),
        in_specs=[
            pl.BlockSpec((tm, d), lambda s: (jnp.minimum(s // nf, nt - 1), 0)),
            pl.BlockSpec((BF16_ROWS, d), lambda s: (0, 0)),
            pl.BlockSpec((d, tf), lambda s: (0, s % nf)),
            pl.BlockSpec((tf, d), lambda s: (prev(s) % nf, 0)),
        ],
        out_specs=pl.BlockSpec((tm, d), lambda s: (prev(s) // nf, 0)),
        out_shape=jax.ShapeDtypeStruct((t, d), F32),
        scratch_shapes=[pltpu.VMEM((tm, d), BF16), pltpu.VMEM((2, tm, tf), BF16)],
        compiler_params=_params(("arbitrary",), MLP_VMEM_LIMIT_BYTES),
        name="mlp",
    )(h, g, wu, wd)


def _ple_kernel(h_ref, hcol_ref, p_ref, wg_ref, wp_ref, o_ref, hb_ref, z_ref, *, tm, nj):
    s = pl.program_id(0)
    slot = s % 2

    @pl.when(s == 0)
    def _():
        z_ref[1] = jnp.zeros(z_ref.shape[1:], z_ref.dtype)

    @pl.when(s % nj == 0)
    def _():
        def body(r, carry):
            rows = pl.ds(pl.multiple_of(r * BF16_ROWS, BF16_ROWS), BF16_ROWS)
            hb_ref[rows, :] = h_ref[rows, :].astype(BF16)
            return carry

        lax.fori_loop(0, tm // BF16_ROWS, body, 0)

    emb = jnp.dot(p_ref[...].astype(BF16), wp_ref[...], preferred_element_type=F32)
    o_ref[...] = hcol_ref[...] + jax.nn.sigmoid(z_ref[1 - slot]) * emb
    z_ref[slot] = jnp.dot(hb_ref[...], wg_ref[...], preferred_element_type=F32)


def _ple(h, p, wg, wp, *, tm, tn):
    t, d = h.shape
    kp = p.shape[1]
    nt, nj = t // tm, d // tn
    prev = lambda s: jnp.maximum(s - 1, 0)
    return pl.pallas_call(
        functools.partial(_ple_kernel, tm=tm, nj=nj),
        grid=(nt * nj + 1,),
        in_specs=[
            pl.BlockSpec((tm, d), lambda s: (jnp.minimum(s // nj, nt - 1), 0)),
            pl.BlockSpec((tm, tn), lambda s: (prev(s) // nj, prev(s) % nj)),
            pl.BlockSpec((tm, kp), lambda s: (prev(s) // nj, 0)),
            pl.BlockSpec((d, tn), lambda s: (0, s % nj)),
            pl.BlockSpec((kp, tn), lambda s: (0, prev(s) % nj)),
        ],
        out_specs=pl.BlockSpec((tm, tn), lambda s: (prev(s) // nj, prev(s) % nj)),
        out_shape=jax.ShapeDtypeStruct((t, d), F32),
        scratch_shapes=[pltpu.VMEM((tm, d), BF16), pltpu.VMEM((2, tm, tn), F32)],
        compiler_params=_params(("arbitrary",)),
        name="ple",
    )(h, h, p, wg, wp)


def _final_norm_kernel(h_ref, g_ref, o_ref, *, tm):
    _rmsnorm_rows(h_ref, g_ref, o_ref, tm)


def _final_norm(h, g, *, tm):
    t, d = h.shape
    return pl.pallas_call(
        functools.partial(_final_norm_kernel, tm=tm),
        grid=(t // tm,),
        in_specs=[pl.BlockSpec((tm, d), lambda i: (i, 0)), pl.BlockSpec((BF16_ROWS, d), lambda i: (0, 0))],
        out_specs=pl.BlockSpec((tm, d), lambda i: (i, 0)),
        out_shape=jax.ShapeDtypeStruct((t, d), F32),
        compiler_params=_params(("parallel",)),
        name="final_norm",
    )(h, g)


def kernel(x, p, norm_mix, w_in, conv_w, conv_b, conv_ln_g, conv_ln_b, sconv_w, attn_sinks, pool_w,
           pool_scale, mix_gain, w_out, norm_mlp, w_up, w_down, w_ple_gate, w_ple_proj, norm_final):
    b, s, d = x.shape
    depth = w_in.shape[0]
    t = b * s
    in_cols = w_in.shape[2]
    tn_in = 1280
    in_cols_padded = -(-in_cols // tn_in) * tn_in
    row = lambda v: v.reshape(1, -1).astype(F32)
    rows16 = lambda v: jnp.broadcast_to(v.reshape(1, -1).astype(F32), (BF16_ROWS, v.shape[-1]))

    tm_small, tm_big = min(512, t), min(1024, t)
    h = x.reshape(t, d)
    for i in range(depth):
        proj = _inproj(h, rows16(norm_mix[i]), _cast_layer_bf16(w_in, i, in_cols_padded),
                       tm=tm_small, tn=tn_in)
        y, (w_out_b, w_up_b, w_down_b, w_gate_b) = _mixers(
            proj.reshape(b, s, in_cols_padded), conv_w[i], row(conv_b[i]), row(conv_ln_g[i]),
            row(conv_ln_b[i]), sconv_w[i], attn_sinks[i], pool_w[i].astype(BF16),
            row(pool_scale[i]), row(mix_gain[i]), d=d, layer=i,
            rider_weights=(w_out, w_up, w_down, w_ple_gate))
        h = _outproj(y.reshape(t, d), w_out_b, h, tm=tm_big, tn=512)
        h = _mlp(h, rows16(norm_mlp[i]), w_up_b, w_down_b, tm=tm_small, tf=512)
        h = _ple(h, p[i].reshape(t, -1), w_gate_b, _cast_layer_bf16(w_ple_proj, i),
                 tm=tm_small, tn=1024)
    out = _final_norm(h, rows16(norm_final), tm=256)
    return out.reshape(b, s, d)
```

```python
import functools

import numpy as np
import jax
import jax.numpy as jnp
from jax import lax
from jax.experimental import pallas as pl
from jax.experimental.pallas import tpu as pltpu

F32 = jnp.float32
BF16 = jnp.bfloat16
EPS = 1e-6

LANES = 128
BF16_ROWS = 16
SUBLANES = 8
VMEM_LIMIT_BYTES = 56 * 1024 * 1024
MLP_VMEM_LIMIT_BYTES = 60 * 1024 * 1024
CAST_BLOCK_BYTES = 4 * 1024 * 1024

CONV_WIDTH = 31
SCONV_WIDTH = 3
HEAD_DIM = 64
N_Q_HEADS = 16
WINDOW = 128
POOL_WINDOWS = (2, 4, 8, 16)
HALO = 32
NEG = -1e30


def _params(sem, vmem_limit_bytes=VMEM_LIMIT_BYTES):
    return pltpu.CompilerParams(dimension_semantics=sem, vmem_limit_bytes=vmem_limit_bytes)


def _rmsnorm_rows(src_ref, g_ref, dst_ref, n_rows, copy_ref=None):
    d = src_ref.shape[1]
    lane_chunks = [slice(c, c + LANES) for c in range(0, d, LANES)]

    def body(r, carry):
        rows = pl.ds(pl.multiple_of(r * BF16_ROWS, BF16_ROWS), BF16_ROWS)
        ss = [jnp.zeros((BF16_ROWS, LANES), F32) for _ in range(4)]
        for n, lanes in enumerate(lane_chunks):
            x = src_ref[rows, lanes]
            ss[n % 4] = ss[n % 4] + x * x
            if copy_ref is not None:
                copy_ref[rows, lanes] = x
        tot = jnp.sum((ss[0] + ss[1]) + (ss[2] + ss[3]), axis=-1, keepdims=True)
        scale = jnp.broadcast_to(lax.rsqrt(tot * (1.0 / d) + EPS), (BF16_ROWS, LANES))
        for lanes in lane_chunks:
            dst_ref[rows, lanes] = (src_ref[rows, lanes] * scale * g_ref[:, lanes]).astype(dst_ref.dtype)
        return carry

    lax.fori_loop(0, n_rows // BF16_ROWS, body, 0, unroll=8)


def _cast_kernel(w_ref, o_ref, *, n):
    o_ref[:, 0:n] = w_ref[...].astype(o_ref.dtype)
    if o_ref.shape[1] > n:
        o_ref[:, n:] = jnp.zeros((o_ref.shape[0], o_ref.shape[1] - n), o_ref.dtype)


def _cast_layer_bf16(w, layer, n_padded=None):
    _, k, n = w.shape
    n_padded = n if n_padded is None else n_padded
    rows_fit = CAST_BLOCK_BYTES // (4 * n)
    bk = min(k, max(BF16_ROWS, 1 << (rows_fit.bit_length() - 1)))
    return pl.pallas_call(
        functools.partial(_cast_kernel, n=n),
        grid=(k // bk,),
        in_specs=[pl.BlockSpec((None, bk, n), lambda r: (layer, r, 0))],
        out_specs=pl.BlockSpec((bk, n_padded), lambda r: (r, 0)),
        out_shape=jax.ShapeDtypeStruct((k, n_padded), BF16),
        compiler_params=_params(("parallel",)),
        name="cast_bf16",
    )(w)


def _inproj_kernel(h_ref, g_ref, w_ref, o_ref, hn_ref, *, tm):
    @pl.when(pl.program_id(1) == 0)
    def _():
        _rmsnorm_rows(h_ref, g_ref, hn_ref, tm)

    o_ref[...] = jnp.dot(hn_ref[...], w_ref[...], preferred_element_type=F32).astype(o_ref.dtype)


def _inproj(h, g, w, *, tm, tn):
    t, d = h.shape
    n = w.shape[1]
    return pl.pallas_call(
        functools.partial(_inproj_kernel, tm=tm),
        grid=(t // tm, n // tn),
        in_specs=[
            pl.BlockSpec((tm, d), lambda i, j: (i, 0)),
            pl.BlockSpec((BF16_ROWS, d), lambda i, j: (0, 0)),
            pl.BlockSpec((d, tn), lambda i, j: (0, j)),
        ],
        out_specs=pl.BlockSpec((tm, tn), lambda i, j: (i, j)),
        out_shape=jax.ShapeDtypeStruct((t, n), BF16),
        scratch_shapes=[pltpu.VMEM((tm, d), BF16)],
        compiler_params=_params(("parallel", "arbitrary")),
        name="inproj",
    )(h, g, w)


N_MIXER_INPUTS = 12


def _mixer_kernel(*refs, ts, ch, n_riders):
    (cur_ref, prev_ref, conv_w_ref, conv_b_ref, ln_g_ref, ln_b_ref, sconv_w_ref,
     sinks_ref, bias_ref, pool_w_ref, pool_scale_ref, gain_ref) = refs[:N_MIXER_INPUTS]
    rider_in = refs[N_MIXER_INPUTS:N_MIXER_INPUTS + n_riders]
    y_ref = refs[N_MIXER_INPUTS + n_riders]
    rider_out = refs[N_MIXER_INPUTS + n_riders + 1:N_MIXER_INPUTS + 2 * n_riders + 1]
    ext_ref, tmp_ref, kd_ref, vd_ref = refs[N_MIXER_INPUTS + 2 * n_riders + 1:]
    ext_a, ext_b, ext_d = ext_ref.at[0], ext_ref.at[1], ext_ref.at[2]
    tmp_a, tmp_b, tmp_c, tmp_d = tmp_ref.at[0], tmp_ref.at[1], tmp_ref.at[2], tmp_ref.at[3]

    i = pl.program_id(1)
    not_first = jnp.where(i > 0, 1.0, 0.0).astype(F32)
    n_lane_chunks = ch // LANES
    col_a, col_b, col_q = 0, 2 * ch, 5 * ch
    col_k = 6 * ch
    col_v = col_k + LANES
    col_d = col_v + LANES
    hrows = slice(ts - HALO, ts)

    def cur(c0, n=LANES):
        return cur_ref[0, :, c0:c0 + n]

    def prev_halo(c0, n=LANES):
        return prev_ref[0, hrows, c0:c0 + n].astype(F32)

    def finish(x_rows, col0, rows):
        ms = jnp.mean(x_rows * x_rows, axis=-1, keepdims=True)
        g = gain_ref[:, col0:col0 + ch]
        y_ref[0, rows, col0:col0 + ch] = (x_rows * lax.rsqrt(ms + EPS) * g).astype(y_ref.dtype)

    sub = 32
    row_chunks = [slice(r, r + sub) for r in range(0, ts, sub)]

    for c in range(n_lane_chunks):
        l0 = c * LANES
        val = cur(col_a + l0).astype(F32)
        gate = cur(col_a + ch + l0).astype(F32)
        ext_a[HALO:HALO + ts, l0:l0 + LANES] = val * jax.nn.sigmoid(gate)
        pval = prev_halo(col_a + l0)
        pgate = prev_halo(col_a + ch + l0)
        ext_a[0:HALO, l0:l0 + LANES] = pval * jax.nn.sigmoid(pgate) * not_first
    first = HALO - (CONV_WIDTH - 1)
    ext_a[HALO + ts:HALO + ts + SUBLANES, :] = jnp.zeros((SUBLANES, ch), F32)
    for c in range(n_lane_chunks):
        l0 = c * LANES
        acc = conv_b_ref[:, l0:l0 + LANES]
        for b in range(SUBLANES):
            q = None
            for a in range(HALO // SUBLANES + 1):
                o = SUBLANES * a + b
                if first <= o <= HALO:
                    term = (ext_a[SUBLANES * a:SUBLANES * a + ts + SUBLANES, l0:l0 + LANES]
                            * conv_w_ref[o - first:o - first + 1, l0:l0 + LANES])
                    q = term if q is None else q + term
            acc = acc + q[b:b + ts]
        tmp_a[:, l0:l0 + LANES] = acc
    for rows in row_chunks:
        x = tmp_a[rows, :]
        mu = jnp.mean(x, axis=-1, keepdims=True)
        xc = x - mu
        var = jnp.mean(xc * xc, axis=-1, keepdims=True)
        ln = xc * lax.rsqrt(var + EPS) * ln_g_ref[...] + ln_b_ref[...]
        finish(ln * jax.nn.sigmoid(ln), 0, rows)

    for c in range(n_lane_chunks):
        l0 = c * LANES
        cg = cur(col_b + ch + l0).astype(F32)
        xin = cur(col_b + 2 * ch + l0).astype(F32)
        ext_b[HALO:HALO + ts, l0:l0 + LANES] = cg * xin
        ext_b[0:HALO, l0:l0 + LANES] = (prev_halo(col_b + ch + l0) * prev_halo(col_b + 2 * ch + l0)
                                        * not_first)
    for c in range(n_lane_chunks):
        l0 = c * LANES
        acc = jnp.zeros((ts, LANES), F32)
        for k in range(SCONV_WIDTH):
            off = HALO - (SCONV_WIDTH - 1) + k
            acc = acc + ext_b[off:off + ts, l0:l0 + LANES] * sconv_w_ref[k:k + 1, l0:l0 + LANES]
        tmp_b[:, l0:l0 + LANES] = cur(col_b + l0).astype(F32) * acc
    for rows in row_chunks:
        finish(tmp_b[rows, :], ch, rows)

    lane = lax.broadcasted_iota(jnp.int32, (2 * ts, LANES), 1)
    lo = lane < HEAD_DIM

    def block_diag(col0, dst_ref, scale):
        both = jnp.concatenate([prev_ref[0, :, col0:col0 + LANES], cur(col0)], axis=0).astype(F32)
        if scale != 1.0:
            both = both * scale
        swapped = pltpu.roll(both, HEAD_DIM, 1)
        zero = jnp.zeros_like(both)
        dst_ref[0, 0:2 * ts, 0:LANES] = jnp.where(lo, both, zero).astype(BF16)
        dst_ref[0, 2 * ts:4 * ts, 0:LANES] = jnp.where(lo, zero, swapped).astype(BF16)
        dst_ref[1, 0:2 * ts, 0:LANES] = jnp.where(lo, swapped, zero).astype(BF16)
        dst_ref[1, 2 * ts:4 * ts, 0:LANES] = jnp.where(lo, zero, both).astype(BF16)

    block_diag(col_k, kd_ref, 1.0 / np.sqrt(HEAD_DIM))
    block_diag(col_v, vd_ref, 1.0)
    ones_a = jnp.where(lo, 1.0, 0.0).astype(BF16)
    ones_b = jnp.where(lo, 0.0, 1.0).astype(BF16)
    for kvh in range(2):
        vd_ref[kvh, 0:2 * ts, LANES:2 * LANES] = ones_a
        vd_ref[kvh, 2 * ts:4 * ts, LANES:2 * LANES] = ones_b
    bias_set = jnp.where(i == 0, 1, 0)
    lo_q = lax.broadcasted_iota(jnp.int32, (ts, LANES), 1) < HEAD_DIM
    heads_per_kv = N_Q_HEADS // 2
    for pair in range(N_Q_HEADS // 2):
        kvh = (2 * pair) // heads_per_kv
        q2 = cur(col_q + pair * LANES)
        s = lax.dot_general(q2, kd_ref[kvh], (((1,), (1,)), ((), ())), preferred_element_type=F32)
        s = s + bias_ref[bias_set, pair]
        probs, sink_e = [], []
        for hh in range(2):
            sh = s[:, hh * 2 * ts:(hh + 1) * 2 * ts]
            sink = sinks_ref[2 * pair + hh]
            m = jnp.maximum(jnp.max(sh, axis=-1, keepdims=True), sink)
            probs.append(jnp.exp(sh - m).astype(BF16))
            sink_e.append(jnp.exp(sink - m))
        o = jnp.dot(jnp.concatenate(probs, axis=1), vd_ref[kvh], preferred_element_type=F32)
        den = o[:, LANES:2 * LANES] + jnp.where(lo_q, sink_e[0], sink_e[1])
        tmp_c[:, pair * LANES:(pair + 1) * LANES] = o[:, 0:LANES] / den
    for rows in row_chunks:
        finish(tmp_c[rows, :], 2 * ch, rows)

    grp = ch // len(POOL_WINDOWS)
    t_glob = i * ts + lax.broadcasted_iota(jnp.int32, (ts, 1), 0)
    for gi, w in enumerate(POOL_WINDOWS):
        l0 = gi * grp
        d_cur = cur(col_d + l0, grp).astype(F32)
        ext_d[HALO:HALO + ts, l0:l0 + grp] = d_cur
        ext_d[0:HALO, l0:l0 + grp] = prev_halo(col_d + l0, grp) * not_first
        shift = 1
        while shift < w:
            hi = ext_d[8:HALO + ts, l0:l0 + grp]
            sh_ = ext_d[8 - shift:HALO + ts - shift, l0:l0 + grp]
            ext_d[8:HALO + ts, l0:l0 + grp] = hi + sh_
            shift *= 2
        count = jnp.minimum(t_glob + 1, w).astype(F32)
        z = ext_d[HALO:HALO + ts, l0:l0 + grp] / count - d_cur
        zz = jnp.dot(z.astype(BF16), pool_w_ref[gi], preferred_element_type=F32)
        tmp_d[:, l0:l0 + grp] = zz * pool_scale_ref[:, l0:l0 + grp]
    for rows in row_chunks:
        finish(tmp_d[rows, :], 3 * ch, rows)

    for w_ref, wb_ref in zip(rider_in, rider_out):
        _cast_kernel(w_ref, wb_ref, n=w_ref.shape[-1])


def _attn_bias_pairs(ts):
    qi = np.arange(ts)[:, None]
    kj = np.arange(2 * ts)[None, :]
    dist = qi + ts - kj
    valid = (dist >= 0) & (dist < WINDOW)
    slopes = np.exp2(-8.0 * np.arange(1, N_Q_HEADS + 1, dtype=np.float32) / N_Q_HEADS).astype(np.float32)
    alibi = -slopes[:, None, None] * dist.astype(np.float32)[None]
    sets = []
    for first_block in (False, True):
        ok = valid & ~((kj < ts) & first_block)
        bias = np.where(ok[None], alibi, np.float32(NEG)).astype(np.float32)
        bias = bias.reshape(N_Q_HEADS // 2, 2, ts, 2 * ts)
        sets.append(np.concatenate([bias[:, 0], bias[:, 1]], axis=-1))
    return np.stack(sets)


def _mixers(proj, conv_w, conv_b, ln_g, ln_b, sconv_w, sinks, pool_w, pool_scale, gain, *, d,
            rider_weights):
    b, s, n = proj.shape
    ts = WINDOW
    ch = d // 4
    nblk = s // ts
    n_steps = b * nblk
    bias = jnp.asarray(_attn_bias_pairs(ts))
    full = lambda shape: pl.BlockSpec(shape, lambda bi, i: (0,) * len(shape))
    rider_in_specs, rider_out_specs, rider_out_shapes = [], [], []
    for w, w_layer, n_padded in rider_weights:
        _, k, wn = w.shape
        rb = k // n_steps
        assert rb * n_steps == k and rb % BF16_ROWS == 0
        rider_in_specs.append(
            pl.BlockSpec((None, rb, wn), lambda bi, i, w_layer=w_layer: (w_layer, bi * nblk + i, 0)))
        rider_out_specs.append(pl.BlockSpec((rb, n_padded), lambda bi, i: (bi * nblk + i, 0)))
        rider_out_shapes.append(jax.ShapeDtypeStruct((k, n_padded), BF16))
    outs = pl.pallas_call(
        functools.partial(_mixer_kernel, ts=ts, ch=ch, n_riders=len(rider_weights)),
        grid=(b, nblk),
        in_specs=[
            pl.BlockSpec((1, ts, n), lambda bi, i: (bi, i, 0)),
            pl.BlockSpec((1, ts, n), lambda bi, i: (bi, jnp.maximum(i - 1, 0), 0)),
            full((CONV_WIDTH, ch)), full((1, ch)), full((1, ch)), full((1, ch)),
            full((SCONV_WIDTH, ch)),
            pl.BlockSpec(memory_space=pltpu.SMEM),
            full(bias.shape),
            full(pool_w.shape), full((1, ch)), full((1, d)),
        ] + rider_in_specs,
        out_specs=[pl.BlockSpec((1, ts, d), lambda bi, i: (bi, i, 0))] + rider_out_specs,
        out_shape=[jax.ShapeDtypeStruct((b, s, d), BF16)] + rider_out_shapes,
        scratch_shapes=[pltpu.VMEM((3, HALO + ts + SUBLANES, ch), F32), pltpu.VMEM((4, ts, ch), F32),
                        pltpu.VMEM((2, 4 * ts, LANES), BF16), pltpu.VMEM((2, 4 * ts, 2 * LANES), BF16)],
        compiler_params=_params(("arbitrary", "arbitrary")),
        name="mixers",
    )(proj, proj, conv_w, conv_b, ln_g, ln_b, sconv_w, sinks, bias, pool_w, pool_scale, gain,
      *[w for w, _, _ in rider_weights])
    return outs[0], outs[1:]


def _outproj_kernel(y_ref, w_ref, h_ref, o_ref):
    o_ref[...] = h_ref[...] + jnp.dot(y_ref[...], w_ref[...], preferred_element_type=F32)


def _outproj(y, w, h, *, tm, tn):
    t, d = h.shape
    k = y.shape[1]
    return pl.pallas_call(
        _outproj_kernel,
        grid=(t // tm, d // tn),
        in_specs=[
            pl.BlockSpec((tm, k), lambda i, j: (i, 0)),
            pl.BlockSpec((k, tn), lambda i, j: (0, j)),
            pl.BlockSpec((tm, tn), lambda i, j: (i, j)),
        ],
        out_specs=pl.BlockSpec((tm, tn), lambda i, j: (i, j)),
        out_shape=jax.ShapeDtypeStruct((t, d), F32),
        compiler_params=_params(("parallel", "arbitrary")),
        name="outproj",
    )(y, w, h)


def _mlp_kernel(h_ref, g_ref, wu_ref, wd_ref, o_ref, hn_ref, up_ref, *, tm, nf):
    s = pl.program_id(0)
    f = s % nf
    slot = s % 2

    @pl.when(s == 0)
    def _():
        up_ref[1] = jnp.zeros(up_ref.shape[1:], up_ref.dtype)
        o_ref[...] = jnp.zeros(o_ref.shape, o_ref.dtype)

    @pl.when(f == 0)
    def _():
        _rmsnorm_rows(h_ref, g_ref, hn_ref, tm)

    @pl.when(f == 1)
    def _():
        o_ref[...] = h_ref[...]

    o_ref[...] += jnp.dot(up_ref[1 - slot], wd_ref[...], preferred_element_type=F32)
    up = jnp.dot(hn_ref[...], wu_ref[...], preferred_element_type=F32)
    up_ref[slot] = jnp.square(jnp.maximum(up, 0.0)).astype(up_ref.dtype)


def _mlp(h, g, wu, wd, *, tm, tf):
    t, d = h.shape
    nf = wu.shape[1] // tf
    nt = t // tm
    assert nf >= 2
    prev = lambda s: jnp.maximum(s - 1, 0)
    return pl.pallas_call(
        functools.partial(_mlp_kernel, tm=tm, nf=nf),
        grid=(nt * nf + 1,),
        in_specs=[
            pl.BlockSpec((tm, d), lambda s: (jnp.minimum(s // nf, nt - 1), 0)),
            pl.BlockSpec((BF16_ROWS, d), lambda s: (0, 0)),
            pl.BlockSpec((d, tf), lambda s: (0, s % nf)),
            pl.BlockSpec((tf, d), lambda s: (prev(s) % nf, 0)),
        ],
        out_specs=pl.BlockSpec((tm, d), lambda s: (prev(s) // nf, 0)),
        out_shape=jax.ShapeDtypeStruct((t, d), F32),
        scratch_shapes=[pltpu.VMEM((tm, d), BF16), pltpu.VMEM((2, tm, tf), BF16)],
        compiler_params=_params(("arbitrary",), MLP_VMEM_LIMIT_BYTES),
        name="mlp",
    )(h, g, wu, wd)


def _ple_kernel(h_ref, p_ref, wg_ref, wp_ref, o_ref, hb_ref, z_ref, res_ref, *, tm, tn, nj):
    s = pl.program_id(0)
    slot = s % 2
    cols = pl.ds(pl.multiple_of((s % nj) * tn, tn), tn)

    @pl.when(s == 0)
    def _():
        z_ref[1] = jnp.zeros(z_ref.shape[1:], z_ref.dtype)
        res_ref[1] = jnp.zeros(res_ref.shape[1:], res_ref.dtype)

    @pl.when(s % nj == 0)
    def _():
        def body(r, carry):
            rows = pl.ds(pl.multiple_of(r * BF16_ROWS, BF16_ROWS), BF16_ROWS)
            hb_ref[rows, :] = h_ref[rows, :].astype(BF16)
            return carry

        lax.fori_loop(0, tm // BF16_ROWS, body, 0)

    emb = jnp.dot(p_ref[...].astype(BF16), wp_ref[...], preferred_element_type=F32)
    o_ref[...] = res_ref[1 - slot] + jax.nn.sigmoid(z_ref[1 - slot]) * emb
    z_ref[slot] = jnp.dot(hb_ref[...], wg_ref[...], preferred_element_type=F32)
    res_ref[slot] = h_ref[:, cols]


def _ple(h, p, wg, wp, *, tm, tn):
    t, d = h.shape
    kp = p.shape[1]
    nt, nj = t // tm, d // tn
    prev = lambda s: jnp.maximum(s - 1, 0)
    return pl.pallas_call(
        functools.partial(_ple_kernel, tm=tm, tn=tn, nj=nj),
        grid=(nt * nj + 1,),
        in_specs=[
            pl.BlockSpec((tm, d), lambda s: (jnp.minimum(s // nj, nt - 1), 0)),
            pl.BlockSpec((tm, kp), lambda s: (prev(s) // nj, 0)),
            pl.BlockSpec((d, tn), lambda s: (0, s % nj)),
            pl.BlockSpec((kp, tn), lambda s: (0, prev(s) % nj)),
        ],
        out_specs=pl.BlockSpec((tm, tn), lambda s: (prev(s) // nj, prev(s) % nj)),
        out_shape=jax.ShapeDtypeStruct((t, d), F32),
        scratch_shapes=[pltpu.VMEM((tm, d), BF16), pltpu.VMEM((2, tm, tn), F32),
                        pltpu.VMEM((2, tm, tn), F32)],
        compiler_params=_params(("arbitrary",)),
        name="ple",
    )(h, p, wg, wp)


def _final_norm_kernel(h_ref, g_ref, o_ref, *, tm):
    _rmsnorm_rows(h_ref, g_ref, o_ref, tm)


def _final_norm(h, g, *, tm):
    t, d = h.shape
    return pl.pallas_call(
        functools.partial(_final_norm_kernel, tm=tm),
        grid=(t // tm,),
        in_specs=[pl.BlockSpec((tm, d), lambda i: (i, 0)), pl.BlockSpec((BF16_ROWS, d), lambda i: (0, 0))],
        out_specs=pl.BlockSpec((tm, d), lambda i: (i, 0)),
        out_shape=jax.ShapeDtypeStruct((t, d), F32),
        compiler_params=_params(("parallel",)),
        name="final_norm",
    )(h, g)


def kernel(x, p, norm_mix, w_in, conv_w, conv_b, conv_ln_g, conv_ln_b, sconv_w, attn_sinks, pool_w,
           pool_scale, mix_gain, w_out, norm_mlp, w_up, w_down, w_ple_gate, w_ple_proj, norm_final):
    b, s, d = x.shape
    depth = w_in.shape[0]
    t = b * s
    in_cols = w_in.shape[2]
    tn_in = 1280
    in_cols_padded = -(-in_cols // tn_in) * tn_in
    row = lambda v: v.reshape(1, -1).astype(F32)
    rows16 = lambda v: jnp.broadcast_to(v.reshape(1, -1).astype(F32), (BF16_ROWS, v.shape[-1]))

    tm_small, tm_big = min(512, t), min(1024, t)
    h = x.reshape(t, d)
    w_in_b = _cast_layer_bf16(w_in, 0, in_cols_padded)
    for i in range(depth):
        proj = _inproj(h, rows16(norm_mix[i]), w_in_b, tm=tm_small, tn=tn_in)
        riders = [(w, i, w.shape[2]) for w in (w_out, w_up, w_down, w_ple_gate)]
        if i + 1 < depth:
            riders.append((w_in, i + 1, in_cols_padded))
        y, casts = _mixers(
            proj.reshape(b, s, in_cols_padded), conv_w[i], row(conv_b[i]), row(conv_ln_g[i]),
            row(conv_ln_b[i]), sconv_w[i], attn_sinks[i], pool_w[i].astype(BF16),
            row(pool_scale[i]), row(mix_gain[i]), d=d, rider_weights=riders)
        w_out_b, w_up_b, w_down_b, w_gate_b = casts[:4]
        if i + 1 < depth:
            w_in_b = casts[4]
        h = _outproj(y.reshape(t, d), w_out_b, h, tm=tm_big, tn=512)
        h = _mlp(h, rows16(norm_mlp[i]), w_up_b, w_down_b, tm=tm_small, tf=512)
        h = _ple(h, p[i].reshape(t, -1), w_gate_b, _cast_layer_bf16(w_ple_proj, i),
                 tm=tm_small, tn=1024)
    out = _final_norm(h, rows16(norm_final), tm=256)
    return out.reshape(b, s, d)
```

```python
import functools

import numpy as np
import jax
import jax.numpy as jnp
from jax import lax
from jax.experimental import pallas as pl
from jax.experimental.pallas import tpu as pltpu

F32 = jnp.float32
BF16 = jnp.bfloat16
EPS = 1e-6

LANES = 128
BF16_ROWS = 16
SUBLANES = 8
VMEM_LIMIT_BYTES = 56 * 1024 * 1024
MLP_VMEM_LIMIT_BYTES = 60 * 1024 * 1024
CAST_BLOCK_BYTES = 4 * 1024 * 1024

CONV_WIDTH = 31
SCONV_WIDTH = 3
HEAD_DIM = 64
N_Q_HEADS = 16
WINDOW = 128
POOL_WINDOWS = (2, 4, 8, 16)
HALO = 32
NORM_CHUNK_ROWS = 32
NEG = -1e30


def _tiles(t, d, in_cols):
    tn_in = 1280
    return dict(tm=min(512, t), tm_out=min(1024, t), tn_in=tn_in,
                in_cols_padded=-(-in_cols // tn_in) * tn_in,
                tn_out=512, tf=512, tn_ple=1024, tm_final=min(256, t))


def _params(sem, vmem_limit_bytes=VMEM_LIMIT_BYTES):
    return pltpu.CompilerParams(dimension_semantics=sem, vmem_limit_bytes=vmem_limit_bytes)


def _rmsnorm_rows(src_ref, g_ref, dst_ref, n_rows, copy_ref=None):
    d = src_ref.shape[1]
    lane_chunks = [slice(c, c + LANES) for c in range(0, d, LANES)]

    def body(r, carry):
        rows = pl.ds(pl.multiple_of(r * BF16_ROWS, BF16_ROWS), BF16_ROWS)
        ss = [jnp.zeros((BF16_ROWS, LANES), F32) for _ in range(4)]
        for n, lanes in enumerate(lane_chunks):
            x = src_ref[rows, lanes]
            ss[n % 4] = ss[n % 4] + x * x
            if copy_ref is not None:
                copy_ref[rows, lanes] = x
        tot = jnp.sum((ss[0] + ss[1]) + (ss[2] + ss[3]), axis=-1, keepdims=True)
        scale = jnp.broadcast_to(lax.rsqrt(tot * (1.0 / d) + EPS), (BF16_ROWS, LANES))
        for lanes in lane_chunks:
            dst_ref[rows, lanes] = (src_ref[rows, lanes] * scale * g_ref[:, lanes]).astype(dst_ref.dtype)
        return carry

    lax.fori_loop(0, n_rows // BF16_ROWS, body, 0, unroll=8)


def _cast_kernel(w_ref, o_ref, *, n):
    o_ref[:, 0:n] = w_ref[...].astype(o_ref.dtype)
    if o_ref.shape[1] > n:
        o_ref[:, n:] = jnp.zeros((o_ref.shape[0], o_ref.shape[1] - n), o_ref.dtype)


def _cast_layer_bf16(w, layer, n_padded=None):
    _, k, n = w.shape
    n_padded = n if n_padded is None else n_padded
    rows_fit = CAST_BLOCK_BYTES // (4 * n)
    bk = min(k, max(BF16_ROWS, 1 << (rows_fit.bit_length() - 1)))
    return pl.pallas_call(
        functools.partial(_cast_kernel, n=n),
        grid=(k // bk,),
        in_specs=[pl.BlockSpec((None, bk, n), lambda r: (layer, r, 0))],
        out_specs=pl.BlockSpec((bk, n_padded), lambda r: (r, 0)),
        out_shape=jax.ShapeDtypeStruct((k, n_padded), BF16),
        compiler_params=_params(("parallel",)),
        name="cast_bf16",
    )(w)


def _inproj_kernel(h_ref, g_ref, w_ref, o_ref, hn_ref, *, tm):
    @pl.when(pl.program_id(1) == 0)
    def _():
        _rmsnorm_rows(h_ref, g_ref, hn_ref, tm)

    o_ref[...] = jnp.dot(hn_ref[...], w_ref[...], preferred_element_type=F32).astype(o_ref.dtype)


def _inproj(h, g, w, *, tm, tn):
    t, d = h.shape
    n = w.shape[1]
    return pl.pallas_call(
        functools.partial(_inproj_kernel, tm=tm),
        grid=(t // tm, n // tn),
        in_specs=[
            pl.BlockSpec((tm, d), lambda i, j: (i, 0)),
            pl.BlockSpec((BF16_ROWS, d), lambda i, j: (0, 0)),
            pl.BlockSpec((d, tn), lambda i, j: (0, j)),
        ],
        out_specs=pl.BlockSpec((tm, tn), lambda i, j: (i, j)),
        out_shape=jax.ShapeDtypeStruct((t, n), BF16),
        scratch_shapes=[pltpu.VMEM((tm, d), BF16)],
        compiler_params=_params(("parallel", "arbitrary")),
        name="inproj",
    )(h, g, w)


N_MIXER_INPUTS = 12


def _mixer_kernel(*refs, ts, ch, n_riders):
    (cur_ref, prev_ref, conv_w_ref, conv_b_ref, ln_g_ref, ln_b_ref, sconv_w_ref,
     sinks_ref, bias_ref, pool_w_ref, pool_scale_ref, gain_ref) = refs[:N_MIXER_INPUTS]
    rider_in = refs[N_MIXER_INPUTS:N_MIXER_INPUTS + n_riders]
    y_ref = refs[N_MIXER_INPUTS + n_riders]
    rider_out = refs[N_MIXER_INPUTS + n_riders + 1:N_MIXER_INPUTS + 2 * n_riders + 1]
    ext_ref, tmp_ref, kd_ref, vd_ref = refs[N_MIXER_INPUTS + 2 * n_riders + 1:]
    ext_a, ext_b, ext_d = ext_ref.at[0], ext_ref.at[1], ext_ref.at[2]
    tmp_a, tmp_b, tmp_c, tmp_d = tmp_ref.at[0], tmp_ref.at[1], tmp_ref.at[2], tmp_ref.at[3]

    i = pl.program_id(1)
    not_first = jnp.where(i > 0, 1.0, 0.0).astype(F32)
    n_lane_chunks = ch // LANES
    col_a, col_b, col_q = 0, 2 * ch, 5 * ch
    col_k = 6 * ch
    col_v = col_k + LANES
    col_d = col_v + LANES
    hrows = slice(ts - HALO, ts)

    def cur(c0, n=LANES):
        return cur_ref[0, :, c0:c0 + n]

    def prev_halo(c0, n=LANES):
        return prev_ref[0, hrows, c0:c0 + n].astype(F32)

    def finish(x_rows, col0, rows):
        ms = jnp.mean(x_rows * x_rows, axis=-1, keepdims=True)
        g = gain_ref[:, col0:col0 + ch]
        y_ref[0, rows, col0:col0 + ch] = (x_rows * lax.rsqrt(ms + EPS) * g).astype(y_ref.dtype)

    row_chunks = [slice(r, r + NORM_CHUNK_ROWS) for r in range(0, ts, NORM_CHUNK_ROWS)]

    for c in range(n_lane_chunks):
        l0 = c * LANES
        val = cur(col_a + l0).astype(F32)
        gate = cur(col_a + ch + l0).astype(F32)
        ext_a[HALO:HALO + ts, l0:l0 + LANES] = val * jax.nn.sigmoid(gate)
        pval = prev_halo(col_a + l0)
        pgate = prev_halo(col_a + ch + l0)
        ext_a[0:HALO, l0:l0 + LANES] = pval * jax.nn.sigmoid(pgate) * not_first
    first = HALO - (CONV_WIDTH - 1)
    ext_a[HALO + ts:HALO + ts + SUBLANES, :] = jnp.zeros((SUBLANES, ch), F32)
    for c in range(n_lane_chunks):
        l0 = c * LANES
        acc = conv_b_ref[:, l0:l0 + LANES]
        for b in range(SUBLANES):
            q = None
            for a in range(HALO // SUBLANES + 1):
                o = SUBLANES * a + b
                if first <= o <= HALO:
                    term = (ext_a[SUBLANES * a:SUBLANES * a + ts + SUBLANES, l0:l0 + LANES]
                            * conv_w_ref[o - first:o - first + 1, l0:l0 + LANES])
                    q = term if q is None else q + term
            acc = acc + q[b:b + ts]
        tmp_a[:, l0:l0 + LANES] = acc
    for rows in row_chunks:
        x = tmp_a[rows, :]
        mu = jnp.mean(x, axis=-1, keepdims=True)
        xc = x - mu
        var = jnp.mean(xc * xc, axis=-1, keepdims=True)
        ln = xc * lax.rsqrt(var + EPS) * ln_g_ref[...] + ln_b_ref[...]
        finish(ln * jax.nn.sigmoid(ln), 0, rows)

    for c in range(n_lane_chunks):
        l0 = c * LANES
        cg = cur(col_b + ch + l0).astype(F32)
        xin = cur(col_b + 2 * ch + l0).astype(F32)
        ext_b[HALO:HALO + ts, l0:l0 + LANES] = cg * xin
        ext_b[0:HALO, l0:l0 + LANES] = (prev_halo(col_b + ch + l0) * prev_halo(col_b + 2 * ch + l0)
                                        * not_first)
    for c in range(n_lane_chunks):
        l0 = c * LANES
        acc = jnp.zeros((ts, LANES), F32)
        for k in range(SCONV_WIDTH):
            off = HALO - (SCONV_WIDTH - 1) + k
            acc = acc + ext_b[off:off + ts, l0:l0 + LANES] * sconv_w_ref[k:k + 1, l0:l0 + LANES]
        tmp_b[:, l0:l0 + LANES] = cur(col_b + l0).astype(F32) * acc
    for rows in row_chunks:
        finish(tmp_b[rows, :], ch, rows)

    lane = lax.broadcasted_iota(jnp.int32, (2 * ts, LANES), 1)
    lo = lane < HEAD_DIM

    def block_diag(col0, dst_ref, scale):
        both = jnp.concatenate([prev_ref[0, :, col0:col0 + LANES], cur(col0)], axis=0).astype(F32)
        if scale != 1.0:
            both = both * scale
        swapped = pltpu.roll(both, HEAD_DIM, 1)
        zero = jnp.zeros_like(both)
        dst_ref[0, 0:2 * ts, 0:LANES] = jnp.where(lo, both, zero).astype(BF16)
        dst_ref[0, 2 * ts:4 * ts, 0:LANES] = jnp.where(lo, zero, swapped).astype(BF16)
        dst_ref[1, 0:2 * ts, 0:LANES] = jnp.where(lo, swapped, zero).astype(BF16)
        dst_ref[1, 2 * ts:4 * ts, 0:LANES] = jnp.where(lo, zero, both).astype(BF16)

    block_diag(col_k, kd_ref, 1.0 / np.sqrt(HEAD_DIM))
    block_diag(col_v, vd_ref, 1.0)
    ones_a = jnp.where(lo, 1.0, 0.0).astype(BF16)
    ones_b = jnp.where(lo, 0.0, 1.0).astype(BF16)
    for kvh in range(2):
        vd_ref[kvh, 0:2 * ts, LANES:2 * LANES] = ones_a
        vd_ref[kvh, 2 * ts:4 * ts, LANES:2 * LANES] = ones_b
    bias_set = jnp.where(i == 0, 1, 0)
    lo_q = lax.broadcasted_iota(jnp.int32, (ts, LANES), 1) < HEAD_DIM
    heads_per_kv = N_Q_HEADS // 2
    for pair in range(N_Q_HEADS // 2):
        kvh = (2 * pair) // heads_per_kv
        q2 = cur(col_q + pair * LANES)
        s = lax.dot_general(q2, kd_ref[kvh], (((1,), (1,)), ((), ())), preferred_element_type=F32)
        s = s + bias_ref[bias_set, pair]
        probs, sink_e = [], []
        for hh in range(2):
            sh = s[:, hh * 2 * ts:(hh + 1) * 2 * ts]
            sink = sinks_ref[2 * pair + hh]
            m = jnp.maximum(jnp.max(sh, axis=-1, keepdims=True), sink)
            probs.append(jnp.exp(sh - m).astype(BF16))
            sink_e.append(jnp.exp(sink - m))
        o = jnp.dot(jnp.concatenate(probs, axis=1), vd_ref[kvh], preferred_element_type=F32)
        den = o[:, LANES:2 * LANES] + jnp.where(lo_q, sink_e[0], sink_e[1])
        tmp_c[:, pair * LANES:(pair + 1) * LANES] = o[:, 0:LANES] / den
    for rows in row_chunks:
        finish(tmp_c[rows, :], 2 * ch, rows)

    grp = ch // len(POOL_WINDOWS)
    t_glob = i * ts + lax.broadcasted_iota(jnp.int32, (ts, 1), 0)
    for gi, w in enumerate(POOL_WINDOWS):
        l0 = gi * grp
        d_cur = cur(col_d + l0, grp).astype(F32)
        ext_d[HALO:HALO + ts, l0:l0 + grp] = d_cur
        ext_d[0:HALO, l0:l0 + grp] = prev_halo(col_d + l0, grp) * not_first
        shift = 1
        while shift < w:
            hi = ext_d[SUBLANES:HALO + ts, l0:l0 + grp]
            sh_ = ext_d[SUBLANES - shift:HALO + ts - shift, l0:l0 + grp]
            ext_d[SUBLANES:HALO + ts, l0:l0 + grp] = hi + sh_
            shift *= 2
        count = jnp.minimum(t_glob + 1, w).astype(F32)
        z = ext_d[HALO:HALO + ts, l0:l0 + grp] / count - d_cur
        zz = jnp.dot(z.astype(BF16), pool_w_ref[gi], preferred_element_type=F32)
        tmp_d[:, l0:l0 + grp] = zz * pool_scale_ref[:, l0:l0 + grp]
    for rows in row_chunks:
        finish(tmp_d[rows, :], 3 * ch, rows)

    for w_ref, wb_ref in zip(rider_in, rider_out):
        _cast_kernel(w_ref, wb_ref, n=w_ref.shape[-1])


def _attn_bias_pairs(ts):
    qi = np.arange(ts)[:, None]
    kj = np.arange(2 * ts)[None, :]
    dist = qi + ts - kj
    valid = (dist >= 0) & (dist < WINDOW)
    slopes = np.exp2(-8.0 * np.arange(1, N_Q_HEADS + 1, dtype=np.float32) / N_Q_HEADS).astype(np.float32)
    alibi = -slopes[:, None, None] * dist.astype(np.float32)[None]
    sets = []
    for first_block in (False, True):
        ok = valid & ~((kj < ts) & first_block)
        bias = np.where(ok[None], alibi, np.float32(NEG)).astype(np.float32)
        bias = bias.reshape(N_Q_HEADS // 2, 2, ts, 2 * ts)
        sets.append(np.concatenate([bias[:, 0], bias[:, 1]], axis=-1))
    return np.stack(sets)


def _mixers(proj, conv_w, conv_b, ln_g, ln_b, sconv_w, sinks, pool_w, pool_scale, gain, *, d,
            rider_weights):
    b, s, n = proj.shape
    ts = WINDOW
    ch = d // 4
    nblk = s // ts
    n_steps = b * nblk
    bias = jnp.asarray(_attn_bias_pairs(ts))
    full = lambda shape: pl.BlockSpec(shape, lambda bi, i: (0,) * len(shape))
    rider_in_specs, rider_out_specs, rider_out_shapes = [], [], []
    for w, w_layer, n_padded in rider_weights:
        _, k, wn = w.shape
        rb = k // n_steps
        assert rb * n_steps == k and rb % BF16_ROWS == 0
        rider_in_specs.append(
            pl.BlockSpec((None, rb, wn), lambda bi, i, w_layer=w_layer: (w_layer, bi * nblk + i, 0)))
        rider_out_specs.append(pl.BlockSpec((rb, n_padded), lambda bi, i: (bi * nblk + i, 0)))
        rider_out_shapes.append(jax.ShapeDtypeStruct((k, n_padded), BF16))
    outs = pl.pallas_call(
        functools.partial(_mixer_kernel, ts=ts, ch=ch, n_riders=len(rider_weights)),
        grid=(b, nblk),
        in_specs=[
            pl.BlockSpec((1, ts, n), lambda bi, i: (bi, i, 0)),
            pl.BlockSpec((1, ts, n), lambda bi, i: (bi, jnp.maximum(i - 1, 0), 0)),
            full((CONV_WIDTH, ch)), full((1, ch)), full((1, ch)), full((1, ch)),
            full((SCONV_WIDTH, ch)),
            pl.BlockSpec(memory_space=pltpu.SMEM),
            full(bias.shape),
            full(pool_w.shape), full((1, ch)), full((1, d)),
        ] + rider_in_specs,
        out_specs=[pl.BlockSpec((1, ts, d), lambda bi, i: (bi, i, 0))] + rider_out_specs,
        out_shape=[jax.ShapeDtypeStruct((b, s, d), BF16)] + rider_out_shapes,
        scratch_shapes=[pltpu.VMEM((3, HALO + ts + SUBLANES, ch), F32), pltpu.VMEM((4, ts, ch), F32),
                        pltpu.VMEM((2, 4 * ts, LANES), BF16), pltpu.VMEM((2, 4 * ts, 2 * LANES), BF16)],
        compiler_params=_params(("arbitrary", "arbitrary")),
        name="mixers",
    )(proj, proj, conv_w, conv_b, ln_g, ln_b, sconv_w, sinks, bias, pool_w, pool_scale, gain,
      *[w for w, _, _ in rider_weights])
    return outs[0], outs[1:]


def _outproj_kernel(y_ref, w_ref, h_ref, o_ref):
    o_ref[...] = h_ref[...] + jnp.dot(y_ref[...], w_ref[...], preferred_element_type=F32)


def _outproj(y, w, h, *, tm, tn):
    t, d = h.shape
    k = y.shape[1]
    return pl.pallas_call(
        _outproj_kernel,
        grid=(t // tm, d // tn),
        in_specs=[
            pl.BlockSpec((tm, k), lambda i, j: (i, 0)),
            pl.BlockSpec((k, tn), lambda i, j: (0, j)),
            pl.BlockSpec((tm, tn), lambda i, j: (i, j)),
        ],
        out_specs=pl.BlockSpec((tm, tn), lambda i, j: (i, j)),
        out_shape=jax.ShapeDtypeStruct((t, d), F32),
        compiler_params=_params(("parallel", "arbitrary")),
        name="outproj",
    )(y, w, h)


def _mlp_kernel(h_ref, g_ref, wu_ref, wd_ref, o_ref, hn_ref, up_ref, *, tm, nf):
    s = pl.program_id(0)
    f = s % nf
    slot = s % 2

    @pl.when(s == 0)
    def _():
        up_ref[1] = jnp.zeros(up_ref.shape[1:], up_ref.dtype)
        o_ref[...] = jnp.zeros(o_ref.shape, o_ref.dtype)

    @pl.when(f == 0)
    def _():
        _rmsnorm_rows(h_ref, g_ref, hn_ref, tm)

    @pl.when(f == 1)
    def _():
        o_ref[...] = h_ref[...]

    o_ref[...] += jnp.dot(up_ref[1 - slot], wd_ref[...], preferred_element_type=F32)
    up = jnp.dot(hn_ref[...], wu_ref[...], preferred_element_type=F32)
    up_ref[slot] = jnp.square(jnp.maximum(up, 0.0)).astype(up_ref.dtype)


def _mlp(h, g, wu, wd, *, tm, tf):
    t, d = h.shape
    nf = wu.shape[1] // tf
    nt = t // tm
    assert nf >= 2
    prev = lambda s: jnp.maximum(s - 1, 0)
    return pl.pallas_call(
        functools.partial(_mlp_kernel, tm=tm, nf=nf),
        grid=(nt * nf + 1,),
        in_specs=[
            pl.BlockSpec((tm, d), lambda s: (jnp.minimum(s // nf, nt - 1), 0)),
            pl.BlockSpec((BF16_ROWS, d), lambda s: (0, 0)),
            pl.BlockSpec((d, tf), lambda s: (0, s % nf)),
            pl.BlockSpec((tf, d), lambda s: (prev(s) % nf, 0)),
        ],
        out_specs=pl.BlockSpec((tm, d), lambda s: (prev(s) // nf, 0)),
        out_shape=jax.ShapeDtypeStruct((t, d), F32),
        scratch_shapes=[pltpu.VMEM((tm, d), BF16), pltpu.VMEM((2, tm, tf), BF16)],
        compiler_params=_params(("arbitrary",), MLP_VMEM_LIMIT_BYTES),
        name="mlp",
    )(h, g, wu, wd)


def _ple_kernel(h_ref, p_ref, wg_ref, wp_ref, o_ref, hb_ref, z_ref, res_ref, *, tm, tn, nj):
    s = pl.program_id(0)
    slot = s % 2
    cols = pl.ds(pl.multiple_of((s % nj) * tn, tn), tn)

    @pl.when(s == 0)
    def _():
        z_ref[1] = jnp.zeros(z_ref.shape[1:], z_ref.dtype)
        res_ref[1] = jnp.zeros(res_ref.shape[1:], res_ref.dtype)

    @pl.when(s % nj == 0)
    def _():
        def body(r, carry):
            rows = pl.ds(pl.multiple_of(r * BF16_ROWS, BF16_ROWS), BF16_ROWS)
            hb_ref[rows, :] = h_ref[rows, :].astype(BF16)
            return carry

        lax.fori_loop(0, tm // BF16_ROWS, body, 0)

    emb = jnp.dot(p_ref[...].astype(BF16), wp_ref[...], preferred_element_type=F32)
    o_ref[...] = res_ref[1 - slot] + jax.nn.sigmoid(z_ref[1 - slot]) * emb
    z_ref[slot] = jnp.dot(hb_ref[...], wg_ref[...], preferred_element_type=F32)
    res_ref[slot] = h_ref[:, cols]


def _ple(h, p, wg, wp, *, tm, tn):
    t, d = h.shape
    kp = p.shape[1]
    nt, nj = t // tm, d // tn
    prev = lambda s: jnp.maximum(s - 1, 0)
    return pl.pallas_call(
        functools.partial(_ple_kernel, tm=tm, tn=tn, nj=nj),
        grid=(nt * nj + 1,),
        in_specs=[
            pl.BlockSpec((tm, d), lambda s: (jnp.minimum(s // nj, nt - 1), 0)),
            pl.BlockSpec((tm, kp), lambda s: (prev(s) // nj, 0)),
            pl.BlockSpec((d, tn), lambda s: (0, s % nj)),
            pl.BlockSpec((kp, tn), lambda s: (0, prev(s) % nj)),
        ],
        out_specs=pl.BlockSpec((tm, tn), lambda s: (prev(s) // nj, prev(s) % nj)),
        out_shape=jax.ShapeDtypeStruct((t, d), F32),
        scratch_shapes=[pltpu.VMEM((tm, d), BF16), pltpu.VMEM((2, tm, tn), F32),
                        pltpu.VMEM((2, tm, tn), F32)],
        compiler_params=_params(("arbitrary",)),
        name="ple",
    )(h, p, wg, wp)


def _final_norm_kernel(h_ref, g_ref, o_ref, *, tm):
    _rmsnorm_rows(h_ref, g_ref, o_ref, tm)


def _final_norm(h, g, *, tm):
    t, d = h.shape
    return pl.pallas_call(
        functools.partial(_final_norm_kernel, tm=tm),
        grid=(t // tm,),
        in_specs=[pl.BlockSpec((tm, d), lambda i: (i, 0)), pl.BlockSpec((BF16_ROWS, d), lambda i: (0, 0))],
        out_specs=pl.BlockSpec((tm, d), lambda i: (i, 0)),
        out_shape=jax.ShapeDtypeStruct((t, d), F32),
        compiler_params=_params(("parallel",)),
        name="final_norm",
    )(h, g)


def kernel(x, p, norm_mix, w_in, conv_w, conv_b, conv_ln_g, conv_ln_b, sconv_w, attn_sinks, pool_w,
           pool_scale, mix_gain, w_out, norm_mlp, w_up, w_down, w_ple_gate, w_ple_proj, norm_final):
    b, s, d = x.shape
    depth = w_in.shape[0]
    t = b * s
    tiles = _tiles(t, d, w_in.shape[2])
    in_cols_padded = tiles["in_cols_padded"]
    row = lambda v: v.reshape(1, -1).astype(F32)
    rows16 = lambda v: jnp.broadcast_to(v.reshape(1, -1).astype(F32), (BF16_ROWS, v.shape[-1]))

    h = x.reshape(t, d)
    w_in_b = _cast_layer_bf16(w_in, 0, in_cols_padded)
    for i in range(depth):
        proj = _inproj(h, rows16(norm_mix[i]), w_in_b, tm=tiles["tm"], tn=tiles["tn_in"])
        riders = [(w, i, w.shape[2]) for w in (w_out, w_up, w_down, w_ple_gate)]
        if i + 1 < depth:
            riders.append((w_in, i + 1, in_cols_padded))
        y, casts = _mixers(
            proj.reshape(b, s, in_cols_padded), conv_w[i], row(conv_b[i]), row(conv_ln_g[i]),
            row(conv_ln_b[i]), sconv_w[i], attn_sinks[i], pool_w[i].astype(BF16),
            row(pool_scale[i]), row(mix_gain[i]), d=d, rider_weights=riders)
        w_out_b, w_up_b, w_down_b, w_gate_b = casts[:4]
        if i + 1 < depth:
            w_in_b = casts[4]
        h = _outproj(y.reshape(t, d), w_out_b, h, tm=tiles["tm_out"], tn=tiles["tn_out"])
        h = _mlp(h, rows16(norm_mlp[i]), w_up_b, w_down_b, tm=tiles["tm"], tf=tiles["tf"])
        h = _ple(h, p[i].reshape(t, -1), w_gate_b, _cast_layer_bf16(w_ple_proj, i),
                 tm=tiles["tm"], tn=tiles["tn_ple"])
    out = _final_norm(h, rows16(norm_final), tm=tiles["tm_final"])
    return out.reshape(b, s, d)
```

```python
import functools

import numpy as np
import jax
import jax.numpy as jnp
from jax import lax
from jax.experimental import pallas as pl
from jax.experimental.pallas import tpu as pltpu

F32 = jnp.float32
BF16 = jnp.bfloat16
EPS = 1e-6

LANES = 128
BF16_ROWS = 16
SUBLANES = 8
VMEM_LIMIT_BYTES = 56 * 1024 * 1024
MLP_VMEM_LIMIT_BYTES = 60 * 1024 * 1024
CAST_BLOCK_BYTES = 4 * 1024 * 1024

CONV_WIDTH = 31
SCONV_WIDTH = 3
HEAD_DIM = 64
N_Q_HEADS = 16
WINDOW = 128
POOL_WINDOWS = (2, 4, 8, 16)
HALO = 32
NORM_CHUNK_ROWS = 32
NEG = -1e30


def _tiles(t, d, in_cols):
    tn_in = 1280
    return dict(tm=min(512, t), tm_out=min(1024, t), tn_in=tn_in,
                in_cols_padded=-(-in_cols // tn_in) * tn_in,
                tn_out=512, tf=512, tn_ple=1024, tn_ple_final=512)


def _params(sem, vmem_limit_bytes=VMEM_LIMIT_BYTES):
    return pltpu.CompilerParams(dimension_semantics=sem, vmem_limit_bytes=vmem_limit_bytes)


def _rmsnorm_rows(src_ref, g_ref, dst_ref, n_rows, copy_ref=None):
    d = src_ref.shape[1]
    lane_chunks = [slice(c, c + LANES) for c in range(0, d, LANES)]

    def body(r, carry):
        rows = pl.ds(pl.multiple_of(r * BF16_ROWS, BF16_ROWS), BF16_ROWS)
        ss = [jnp.zeros((BF16_ROWS, LANES), F32) for _ in range(4)]
        for n, lanes in enumerate(lane_chunks):
            x = src_ref[rows, lanes]
            ss[n % 4] = ss[n % 4] + x * x
            if copy_ref is not None:
                copy_ref[rows, lanes] = x
        tot = jnp.sum((ss[0] + ss[1]) + (ss[2] + ss[3]), axis=-1, keepdims=True)
        scale = jnp.broadcast_to(lax.rsqrt(tot * (1.0 / d) + EPS), (BF16_ROWS, LANES))
        for lanes in lane_chunks:
            dst_ref[rows, lanes] = (src_ref[rows, lanes] * scale * g_ref[:, lanes]).astype(dst_ref.dtype)
        return carry

    lax.fori_loop(0, n_rows // BF16_ROWS, body, 0, unroll=8)


def _cast_kernel(w_ref, o_ref, *, n):
    o_ref[:, 0:n] = w_ref[...].astype(o_ref.dtype)
    if o_ref.shape[1] > n:
        o_ref[:, n:] = jnp.zeros((o_ref.shape[0], o_ref.shape[1] - n), o_ref.dtype)


def _cast_layer_bf16(w, layer, n_padded=None):
    _, k, n = w.shape
    n_padded = n if n_padded is None else n_padded
    rows_fit = CAST_BLOCK_BYTES // (4 * n)
    bk = min(k, max(BF16_ROWS, 1 << (rows_fit.bit_length() - 1)))
    return pl.pallas_call(
        functools.partial(_cast_kernel, n=n),
        grid=(k // bk,),
        in_specs=[pl.BlockSpec((None, bk, n), lambda r: (layer, r, 0))],
        out_specs=pl.BlockSpec((bk, n_padded), lambda r: (r, 0)),
        out_shape=jax.ShapeDtypeStruct((k, n_padded), BF16),
        compiler_params=_params(("parallel",)),
        name="cast_bf16",
    )(w)


def _inproj_kernel(h_ref, g_ref, w_ref, o_ref, hn_ref, *, tm):
    @pl.when(pl.program_id(1) == 0)
    def _():
        _rmsnorm_rows(h_ref, g_ref, hn_ref, tm)

    o_ref[...] = jnp.dot(hn_ref[...], w_ref[...], preferred_element_type=F32).astype(o_ref.dtype)


def _inproj(h, g, w, *, tm, tn):
    t, d = h.shape
    n = w.shape[1]
    return pl.pallas_call(
        functools.partial(_inproj_kernel, tm=tm),
        grid=(t // tm, n // tn),
        in_specs=[
            pl.BlockSpec((tm, d), lambda i, j: (i, 0)),
            pl.BlockSpec((BF16_ROWS, d), lambda i, j: (0, 0)),
            pl.BlockSpec((d, tn), lambda i, j: (0, j)),
        ],
        out_specs=pl.BlockSpec((tm, tn), lambda i, j: (i, j)),
        out_shape=jax.ShapeDtypeStruct((t, n), BF16),
        scratch_shapes=[pltpu.VMEM((tm, d), BF16)],
        compiler_params=_params(("parallel", "arbitrary")),
        name="inproj",
    )(h, g, w)


N_MIXER_INPUTS = 12


def _mixer_kernel(*refs, ts, ch, n_riders):
    (cur_ref, prev_ref, conv_w_ref, conv_b_ref, ln_g_ref, ln_b_ref, sconv_w_ref,
     sinks_ref, bias_ref, pool_w_ref, pool_scale_ref, gain_ref) = refs[:N_MIXER_INPUTS]
    rider_in = refs[N_MIXER_INPUTS:N_MIXER_INPUTS + n_riders]
    y_ref = refs[N_MIXER_INPUTS + n_riders]
    rider_out = refs[N_MIXER_INPUTS + n_riders + 1:N_MIXER_INPUTS + 2 * n_riders + 1]
    ext_ref, tmp_ref, kd_ref, vd_ref = refs[N_MIXER_INPUTS + 2 * n_riders + 1:]
    ext_a, ext_b, ext_d = ext_ref.at[0], ext_ref.at[1], ext_ref.at[2]
    tmp_a, tmp_b, tmp_c, tmp_d = tmp_ref.at[0], tmp_ref.at[1], tmp_ref.at[2], tmp_ref.at[3]

    i = pl.program_id(1)
    not_first = jnp.where(i > 0, 1.0, 0.0).astype(F32)
    n_lane_chunks = ch // LANES
    col_a, col_b, col_q = 0, 2 * ch, 5 * ch
    col_k = 6 * ch
    col_v = col_k + LANES
    col_d = col_v + LANES
    hrows = slice(ts - HALO, ts)

    def cur(c0, n=LANES):
        return cur_ref[0, :, c0:c0 + n]

    def prev_halo(c0, n=LANES):
        return prev_ref[0, hrows, c0:c0 + n].astype(F32)

    def finish(x_rows, col0, rows):
        ms = jnp.mean(x_rows * x_rows, axis=-1, keepdims=True)
        g = gain_ref[:, col0:col0 + ch]
        y_ref[0, rows, col0:col0 + ch] = (x_rows * lax.rsqrt(ms + EPS) * g).astype(y_ref.dtype)

    row_chunks = [slice(r, r + NORM_CHUNK_ROWS) for r in range(0, ts, NORM_CHUNK_ROWS)]

    for c in range(n_lane_chunks):
        l0 = c * LANES
        val = cur(col_a + l0).astype(F32)
        gate = cur(col_a + ch + l0).astype(F32)
        ext_a[HALO:HALO + ts, l0:l0 + LANES] = val * jax.nn.sigmoid(gate)
        pval = prev_halo(col_a + l0)
        pgate = prev_halo(col_a + ch + l0)
        ext_a[0:HALO, l0:l0 + LANES] = pval * jax.nn.sigmoid(pgate) * not_first
    first = HALO - (CONV_WIDTH - 1)
    ext_a[HALO + ts:HALO + ts + SUBLANES, :] = jnp.zeros((SUBLANES, ch), F32)
    for c in range(n_lane_chunks):
        l0 = c * LANES
        acc = conv_b_ref[:, l0:l0 + LANES]
        for b in range(SUBLANES):
            q = None
            for a in range(HALO // SUBLANES + 1):
                o = SUBLANES * a + b
                if first <= o <= HALO:
                    term = (ext_a[SUBLANES * a:SUBLANES * a + ts + SUBLANES, l0:l0 + LANES]
                            * conv_w_ref[o - first:o - first + 1, l0:l0 + LANES])
                    q = term if q is None else q + term
            acc = acc + q[b:b + ts]
        tmp_a[:, l0:l0 + LANES] = acc
    for rows in row_chunks:
        x = tmp_a[rows, :]
        mu = jnp.mean(x, axis=-1, keepdims=True)
        xc = x - mu
        var = jnp.mean(xc * xc, axis=-1, keepdims=True)
        ln = xc * lax.rsqrt(var + EPS) * ln_g_ref[...] + ln_b_ref[...]
        finish(ln * jax.nn.sigmoid(ln), 0, rows)

    for c in range(n_lane_chunks):
        l0 = c * LANES
        cg = cur(col_b + ch + l0).astype(F32)
        xin = cur(col_b + 2 * ch + l0).astype(F32)
        ext_b[HALO:HALO + ts, l0:l0 + LANES] = cg * xin
        ext_b[0:HALO, l0:l0 + LANES] = (prev_halo(col_b + ch + l0) * prev_halo(col_b + 2 * ch + l0)
                                        * not_first)
    for c in range(n_lane_chunks):
        l0 = c * LANES
        acc = jnp.zeros((ts, LANES), F32)
        for k in range(SCONV_WIDTH):
            off = HALO - (SCONV_WIDTH - 1) + k
            acc = acc + ext_b[off:off + ts, l0:l0 + LANES] * sconv_w_ref[k:k + 1, l0:l0 + LANES]
        tmp_b[:, l0:l0 + LANES] = cur(col_b + l0).astype(F32) * acc
    for rows in row_chunks:
        finish(tmp_b[rows, :], ch, rows)

    lane = lax.broadcasted_iota(jnp.int32, (2 * ts, LANES), 1)
    lo = lane < HEAD_DIM

    def block_diag(col0, dst_ref, scale):
        both = jnp.concatenate([prev_ref[0, :, col0:col0 + LANES], cur(col0)], axis=0).astype(F32)
        if scale != 1.0:
            both = both * scale
        swapped = pltpu.roll(both, HEAD_DIM, 1)
        zero = jnp.zeros_like(both)
        dst_ref[0, 0:2 * ts, 0:LANES] = jnp.where(lo, both, zero).astype(BF16)
        dst_ref[0, 2 * ts:4 * ts, 0:LANES] = jnp.where(lo, zero, swapped).astype(BF16)
        dst_ref[1, 0:2 * ts, 0:LANES] = jnp.where(lo, swapped, zero).astype(BF16)
        dst_ref[1, 2 * ts:4 * ts, 0:LANES] = jnp.where(lo, zero, both).astype(BF16)

    block_diag(col_k, kd_ref, 1.0 / np.sqrt(HEAD_DIM))
    block_diag(col_v, vd_ref, 1.0)
    ones_a = jnp.where(lo, 1.0, 0.0).astype(BF16)
    ones_b = jnp.where(lo, 0.0, 1.0).astype(BF16)
    for kvh in range(2):
        vd_ref[kvh, 0:2 * ts, LANES:2 * LANES] = ones_a
        vd_ref[kvh, 2 * ts:4 * ts, LANES:2 * LANES] = ones_b
    bias_set = jnp.where(i == 0, 1, 0)
    lo_q = lax.broadcasted_iota(jnp.int32, (ts, LANES), 1) < HEAD_DIM
    heads_per_kv = N_Q_HEADS // 2
    for pair in range(N_Q_HEADS // 2):
        kvh = (2 * pair) // heads_per_kv
        q2 = cur(col_q + pair * LANES)
        s = lax.dot_general(q2, kd_ref[kvh], (((1,), (1,)), ((), ())), preferred_element_type=F32)
        s = s + bias_ref[bias_set, pair]
        probs, sink_e = [], []
        for hh in range(2):
            sh = s[:, hh * 2 * ts:(hh + 1) * 2 * ts]
            sink = sinks_ref[2 * pair + hh]
            m = jnp.maximum(jnp.max(sh, axis=-1, keepdims=True), sink)
            probs.append(jnp.exp(sh - m).astype(BF16))
            sink_e.append(jnp.exp(sink - m))
        o = jnp.dot(jnp.concatenate(probs, axis=1), vd_ref[kvh], preferred_element_type=F32)
        den = o[:, LANES:2 * LANES] + jnp.where(lo_q, sink_e[0], sink_e[1])
        tmp_c[:, pair * LANES:(pair + 1) * LANES] = o[:, 0:LANES] / den
    for rows in row_chunks:
        finish(tmp_c[rows, :], 2 * ch, rows)

    grp = ch // len(POOL_WINDOWS)
    t_glob = i * ts + lax.broadcasted_iota(jnp.int32, (ts, 1), 0)
    for gi, w in enumerate(POOL_WINDOWS):
        l0 = gi * grp
        d_cur = cur(col_d + l0, grp).astype(F32)
        ext_d[HALO:HALO + ts, l0:l0 + grp] = d_cur
        ext_d[0:HALO, l0:l0 + grp] = prev_halo(col_d + l0, grp) * not_first
        shift = 1
        while shift < w:
            hi = ext_d[SUBLANES:HALO + ts, l0:l0 + grp]
            sh_ = ext_d[SUBLANES - shift:HALO + ts - shift, l0:l0 + grp]
            ext_d[SUBLANES:HALO + ts, l0:l0 + grp] = hi + sh_
            shift *= 2
        count = jnp.minimum(t_glob + 1, w).astype(F32)
        z = ext_d[HALO:HALO + ts, l0:l0 + grp] / count - d_cur
        zz = jnp.dot(z.astype(BF16), pool_w_ref[gi], preferred_element_type=F32)
        tmp_d[:, l0:l0 + grp] = zz * pool_scale_ref[:, l0:l0 + grp]
    for rows in row_chunks:
        finish(tmp_d[rows, :], 3 * ch, rows)

    for w_ref, wb_ref in zip(rider_in, rider_out):
        _cast_kernel(w_ref, wb_ref, n=w_ref.shape[-1])


def _attn_bias_pairs(ts):
    qi = np.arange(ts)[:, None]
    kj = np.arange(2 * ts)[None, :]
    dist = qi + ts - kj
    valid = (dist >= 0) & (dist < WINDOW)
    slopes = np.exp2(-8.0 * np.arange(1, N_Q_HEADS + 1, dtype=np.float32) / N_Q_HEADS).astype(np.float32)
    alibi = -slopes[:, None, None] * dist.astype(np.float32)[None]
    sets = []
    for first_block in (False, True):
        ok = valid & ~((kj < ts) & first_block)
        bias = np.where(ok[None], alibi, np.float32(NEG)).astype(np.float32)
        bias = bias.reshape(N_Q_HEADS // 2, 2, ts, 2 * ts)
        sets.append(np.concatenate([bias[:, 0], bias[:, 1]], axis=-1))
    return np.stack(sets)


def _mixers(proj, conv_w, conv_b, ln_g, ln_b, sconv_w, sinks, pool_w, pool_scale, gain, *, d,
            rider_weights):
    b, s, n = proj.shape
    ts = WINDOW
    ch = d // 4
    nblk = s // ts
    n_steps = b * nblk
    bias = jnp.asarray(_attn_bias_pairs(ts))
    full = lambda shape: pl.BlockSpec(shape, lambda bi, i: (0,) * len(shape))
    rider_in_specs, rider_out_specs, rider_out_shapes = [], [], []
    for w, w_layer, n_padded in rider_weights:
        _, k, wn = w.shape
        rb = k // n_steps
        assert rb * n_steps == k and rb % BF16_ROWS == 0
        rider_in_specs.append(
            pl.BlockSpec((None, rb, wn), lambda bi, i, w_layer=w_layer: (w_layer, bi * nblk + i, 0)))
        rider_out_specs.append(pl.BlockSpec((rb, n_padded), lambda bi, i: (bi * nblk + i, 0)))
        rider_out_shapes.append(jax.ShapeDtypeStruct((k, n_padded), BF16))
    outs = pl.pallas_call(
        functools.partial(_mixer_kernel, ts=ts, ch=ch, n_riders=len(rider_weights)),
        grid=(b, nblk),
        in_specs=[
            pl.BlockSpec((1, ts, n), lambda bi, i: (bi, i, 0)),
            pl.BlockSpec((1, ts, n), lambda bi, i: (bi, jnp.maximum(i - 1, 0), 0)),
            full((CONV_WIDTH, ch)), full((1, ch)), full((1, ch)), full((1, ch)),
            full((SCONV_WIDTH, ch)),
            pl.BlockSpec(memory_space=pltpu.SMEM),
            full(bias.shape),
            full(pool_w.shape), full((1, ch)), full((1, d)),
        ] + rider_in_specs,
        out_specs=[pl.BlockSpec((1, ts, d), lambda bi, i: (bi, i, 0))] + rider_out_specs,
        out_shape=[jax.ShapeDtypeStruct((b, s, d), BF16)] + rider_out_shapes,
        scratch_shapes=[pltpu.VMEM((3, HALO + ts + SUBLANES, ch), F32), pltpu.VMEM((4, ts, ch), F32),
                        pltpu.VMEM((2, 4 * ts, LANES), BF16), pltpu.VMEM((2, 4 * ts, 2 * LANES), BF16)],
        compiler_params=_params(("arbitrary", "arbitrary")),
        name="mixers",
    )(proj, proj, conv_w, conv_b, ln_g, ln_b, sconv_w, sinks, bias, pool_w, pool_scale, gain,
      *[w for w, _, _ in rider_weights])
    return outs[0], outs[1:]


def _outproj_kernel(y_ref, w_ref, h_ref, o_ref):
    o_ref[...] = h_ref[...] + jnp.dot(y_ref[...], w_ref[...], preferred_element_type=F32)


def _outproj(y, w, h, *, tm, tn):
    t, d = h.shape
    k = y.shape[1]
    return pl.pallas_call(
        _outproj_kernel,
        grid=(t // tm, d // tn),
        in_specs=[
            pl.BlockSpec((tm, k), lambda i, j: (i, 0)),
            pl.BlockSpec((k, tn), lambda i, j: (0, j)),
            pl.BlockSpec((tm, tn), lambda i, j: (i, j)),
        ],
        out_specs=pl.BlockSpec((tm, tn), lambda i, j: (i, j)),
        out_shape=jax.ShapeDtypeStruct((t, d), F32),
        compiler_params=_params(("parallel", "arbitrary")),
        name="outproj",
    )(y, w, h)


def _mlp_kernel(h_ref, g_ref, wu_ref, wd_ref, o_ref, hn_ref, up_ref, *, tm, nf):
    s = pl.program_id(0)
    f = s % nf
    slot = s % 2

    @pl.when(s == 0)
    def _():
        up_ref[1] = jnp.zeros(up_ref.shape[1:], up_ref.dtype)
        o_ref[...] = jnp.zeros(o_ref.shape, o_ref.dtype)

    @pl.when(f == 0)
    def _():
        _rmsnorm_rows(h_ref, g_ref, hn_ref, tm)

    @pl.when(f == 1)
    def _():
        o_ref[...] = h_ref[...]

    o_ref[...] += jnp.dot(up_ref[1 - slot], wd_ref[...], preferred_element_type=F32)
    up = jnp.dot(hn_ref[...], wu_ref[...], preferred_element_type=F32)
    up_ref[slot] = jnp.square(jnp.maximum(up, 0.0)).astype(up_ref.dtype)


def _mlp(h, g, wu, wd, *, tm, tf):
    t, d = h.shape
    nf = wu.shape[1] // tf
    nt = t // tm
    assert nf >= 2
    prev = lambda s: jnp.maximum(s - 1, 0)
    return pl.pallas_call(
        functools.partial(_mlp_kernel, tm=tm, nf=nf),
        grid=(nt * nf + 1,),
        in_specs=[
            pl.BlockSpec((tm, d), lambda s: (jnp.minimum(s // nf, nt - 1), 0)),
            pl.BlockSpec((BF16_ROWS, d), lambda s: (0, 0)),
            pl.BlockSpec((d, tf), lambda s: (0, s % nf)),
            pl.BlockSpec((tf, d), lambda s: (prev(s) % nf, 0)),
        ],
        out_specs=pl.BlockSpec((tm, d), lambda s: (prev(s) // nf, 0)),
        out_shape=jax.ShapeDtypeStruct((t, d), F32),
        scratch_shapes=[pltpu.VMEM((tm, d), BF16), pltpu.VMEM((2, tm, tf), BF16)],
        compiler_params=_params(("arbitrary",), MLP_VMEM_LIMIT_BYTES),
        name="mlp",
    )(h, g, wu, wd)


def _ple_kernel(*refs, tm, tn, nj, final):
    if final:
        h_ref, p_ref, wg_ref, wp_ref, g_ref, o_ref, hb_ref, z_ref, res_ref, rows_ref = refs
    else:
        h_ref, p_ref, wg_ref, wp_ref, o_ref, hb_ref, z_ref, res_ref = refs
    s = pl.program_id(0)
    slot = s % 2
    cols = pl.ds(pl.multiple_of((s % nj) * tn, tn), tn)

    @pl.when(s == 0)
    def _():
        z_ref[1] = jnp.zeros(z_ref.shape[1:], z_ref.dtype)
        res_ref[1] = jnp.zeros(res_ref.shape[1:], res_ref.dtype)

    @pl.when(s % nj == 0)
    def _():
        def body(r, carry):
            rows = pl.ds(pl.multiple_of(r * BF16_ROWS, BF16_ROWS), BF16_ROWS)
            hb_ref[rows, :] = h_ref[rows, :].astype(BF16)
            return carry

        lax.fori_loop(0, tm // BF16_ROWS, body, 0)

    emb = jnp.dot(p_ref[...].astype(BF16), wp_ref[...], preferred_element_type=F32)
    new_cols = res_ref[1 - slot] + jax.nn.sigmoid(z_ref[1 - slot]) * emb
    if final:
        prev_cols = pl.ds(pl.multiple_of(((s + nj - 1) % nj) * tn, tn), tn)
        rows_ref[:, prev_cols] = new_cols
    else:
        o_ref[...] = new_cols
    z_ref[slot] = jnp.dot(hb_ref[...], wg_ref[...], preferred_element_type=F32)
    res_ref[slot] = h_ref[:, cols]

    if final:
        @pl.when(jnp.logical_and(s % nj == 0, s > 0))
        def _():
            _rmsnorm_rows(rows_ref, g_ref, o_ref, tm)


def _ple(h, p, wg, wp, *, tm, tn, final_gain=None):
    t, d = h.shape
    kp = p.shape[1]
    nt, nj = t // tm, d // tn
    final = final_gain is not None
    prev = lambda s: jnp.maximum(s - 1, 0)
    in_specs = [
        pl.BlockSpec((tm, d), lambda s: (jnp.minimum(s // nj, nt - 1), 0)),
        pl.BlockSpec((tm, kp), lambda s: (prev(s) // nj, 0)),
        pl.BlockSpec((d, tn), lambda s: (0, s % nj)),
        pl.BlockSpec((kp, tn), lambda s: (0, prev(s) % nj)),
    ]
    if final:
        in_specs.append(pl.BlockSpec((BF16_ROWS, d), lambda s: (0, 0)))
        out_spec = pl.BlockSpec((tm, d), lambda s: (prev(s) // nj, 0))
    else:
        out_spec = pl.BlockSpec((tm, tn), lambda s: (prev(s) // nj, prev(s) % nj))
    return pl.pallas_call(
        functools.partial(_ple_kernel, tm=tm, tn=tn, nj=nj, final=final),
        grid=(nt * nj + 1,),
        in_specs=in_specs,
        out_specs=out_spec,
        out_shape=jax.ShapeDtypeStruct((t, d), F32),
        scratch_shapes=[pltpu.VMEM((tm, d), BF16), pltpu.VMEM((2, tm, tn), F32),
                        pltpu.VMEM((2, tm, tn), F32)] + ([pltpu.VMEM((tm, d), F32)] if final else []),
        compiler_params=_params(("arbitrary",), MLP_VMEM_LIMIT_BYTES if final else VMEM_LIMIT_BYTES),
        name="ple",
    )(*((h, p, wg, wp, final_gain) if final else (h, p, wg, wp)))


def kernel(x, p, norm_mix, w_in, conv_w, conv_b, conv_ln_g, conv_ln_b, sconv_w, attn_sinks, pool_w,
           pool_scale, mix_gain, w_out, norm_mlp, w_up, w_down, w_ple_gate, w_ple_proj, norm_final):
    b, s, d = x.shape
    depth = w_in.shape[0]
    t = b * s
    tiles = _tiles(t, d, w_in.shape[2])
    in_cols_padded = tiles["in_cols_padded"]
    row = lambda v: v.reshape(1, -1).astype(F32)
    rows16 = lambda v: jnp.broadcast_to(v.reshape(1, -1).astype(F32), (BF16_ROWS, v.shape[-1]))

    h = x.reshape(t, d)
    w_in_b = _cast_layer_bf16(w_in, 0, in_cols_padded)
    for i in range(depth):
        proj = _inproj(h, rows16(norm_mix[i]), w_in_b, tm=tiles["tm"], tn=tiles["tn_in"])
        riders = [(w, i, w.shape[2]) for w in (w_out, w_up, w_down, w_ple_gate)]
        if i + 1 < depth:
            riders.append((w_in, i + 1, in_cols_padded))
        y, casts = _mixers(
            proj.reshape(b, s, in_cols_padded), conv_w[i], row(conv_b[i]), row(conv_ln_g[i]),
            row(conv_ln_b[i]), sconv_w[i], attn_sinks[i], pool_w[i].astype(BF16),
            row(pool_scale[i]), row(mix_gain[i]), d=d, rider_weights=riders)
        w_out_b, w_up_b, w_down_b, w_gate_b = casts[:4]
        if i + 1 < depth:
            w_in_b = casts[4]
        h = _outproj(y.reshape(t, d), w_out_b, h, tm=tiles["tm_out"], tn=tiles["tn_out"])
        h = _mlp(h, rows16(norm_mlp[i]), w_up_b, w_down_b, tm=tiles["tm"], tf=tiles["tf"])
        last = i + 1 == depth
        h = _ple(h, p[i].reshape(t, -1), w_gate_b, _cast_layer_bf16(w_ple_proj, i), tm=tiles["tm"],
                 tn=tiles["tn_ple_final"] if last else tiles["tn_ple"],
                 final_gain=rows16(norm_final) if last else None)
    return h.reshape(b, s, d)
```
